```python
import math
import jax, jax.numpy as jnp
from jax import lax
import numpy as np

D_MODEL = 1024
BATCH = 16
SEQ = 4096
DEPTH = 4
DEC_BATCH = 16
DEC_SEQ = 64
PAST_LEN = 4096

CHUNK = 64
Q_BLOCK = 128
D_MIX = D_MODEL
D_RWKV = D_MIX // 2
D_DIFF = D_MIX - D_RWKV
RWKV_HEAD = 64
RWKV_HEADS = D_RWKV // RWKV_HEAD
DIFF_HD = 64
DIFF_HEADS = D_DIFF // (2 * DIFF_HD)
DECAY_LORA = 64
AAA_LORA = 64
GATE_LORA = 128
N_SHIFT = 3 * D_RWKV + DECAY_LORA + AAA_LORA + GATE_LORA
N_IN = N_SHIFT + 3 * D_DIFF
D_FF = 2816
CONV_W = 3
NORM_EPS = 1e-6
GN_EPS = 64e-5

kernel_name = 'hymba_rwkv7_diffattn_convffn_stream_step'


def rms_norm(x, g):
    xf = x.astype(jnp.float32)
    y = xf * lax.rsqrt(jnp.mean(xf * xf, axis=-1, keepdims=True) + NORM_EPS)
    return (y * g.astype(jnp.float32)).astype(x.dtype)


def lambda_init(layer):
    return 0.8 - 0.6 * math.exp(-0.3 * layer)


def wkv_scan(s0, r, w, k, v, a, b):
    tm = lambda t: jnp.moveaxis(t.astype(jnp.float32), 1, 0)

    def step(S, inp):
        r_t, w_t, k_t, v_t, a_t, b_t = inp
        sa = jnp.einsum('bhvk,bhk->bhv', S, a_t)
        S = S * w_t[:, :, None, :] + sa[..., None] * b_t[:, :, None, :] + v_t[..., None] * k_t[:, :, None, :]
        return S, jnp.einsum('bhvk,bhk->bhv', S, r_t)

    S, y = lax.scan(step, s0.astype(jnp.float32), (tm(r), tm(w), tm(k), tm(v), tm(a), tm(b)))
    return S, jnp.moveaxis(y, 0, 1)


def rwkv_time_mix(p_s, shift_prev, wkv_prev, mu, w0, w_decay, a0, w_aaa, w_gate, k_k, k_a, r_k, gn_w, gn_b):
    B, T, _ = p_s.shape
    C = D_RWKV
    prev = jnp.concatenate([shift_prev[:, None, :].astype(p_s.dtype), p_s[:, :-1]], axis=1)
    xs = p_s + (prev - p_s) * mu
    r, k, v = xs[..., :C], xs[..., C:2 * C], xs[..., 2 * C:3 * C]
    o = 3 * C
    xw = xs[..., o:o + DECAY_LORA]
    o += DECAY_LORA
    xa = xs[..., o:o + AAA_LORA]
    o += AAA_LORA
    xg = xs[..., o:o + GATE_LORA]
    w = -jax.nn.softplus(-(w0 + jnp.tanh(xw) @ w_decay)) - 0.5
    decay = jnp.exp(-jnp.exp(w.astype(jnp.float32)))
    a = jax.nn.sigmoid(a0 + xa @ w_aaa)
    g = jax.nn.sigmoid(xg) @ w_gate
    hs = lambda t: t.reshape(B, T, RWKV_HEADS, RWKV_HEAD)
    kk = hs((k * k_k).astype(jnp.float32))
    kk = kk / jnp.maximum(jnp.sqrt(jnp.sum(kk * kk, axis=-1, keepdims=True)), 1e-12)
    k = k * (1 + (a - 1) * k_a)
    r_h, k_h, v_h, a_h = hs(r), hs(k), hs(v), hs(a)
    wkv_new, y = wkv_scan(wkv_prev, r_h, hs(decay), k_h, v_h, -kk, kk * a_h.astype(jnp.float32))
    mean = jnp.mean(y, axis=-1, keepdims=True)
    var = jnp.mean(jnp.square(y - mean), axis=-1, keepdims=True)
    y = ((y - mean) * lax.rsqrt(var + GN_EPS)).reshape(B, T, C) * gn_w + gn_b
    bonus = jnp.sum(r_h * k_h * r_k, axis=-1, keepdims=True) * v_h
    y = y + bonus.reshape(B, T, C)
    return (y * g).astype(p_s.dtype), p_s[:, -1], wkv_new


def diff_qkv(p_a, q_gain, k_gain):
    B, T, _ = p_a.shape
    q = p_a[..., :D_DIFF].reshape(B, T, DIFF_HEADS, 2, DIFF_HD)
    k = p_a[..., D_DIFF:2 * D_DIFF].reshape(B, T, DIFF_HEADS, 2, DIFF_HD)
    v = p_a[..., 2 * D_DIFF:].reshape(B, T, DIFF_HEADS, 2 * DIFF_HD)
    return rms_norm(q, q_gain), rms_norm(k, k_gain), v


def diff_lambda(lv, lam_init):
    lv = lv.astype(jnp.float32)
    return jnp.exp(jnp.sum(lv[0] * lv[1])) - jnp.exp(jnp.sum(lv[2] * lv[3])) + lam_init


def diff_core(q, k, v, lam, mask):
    s = jnp.einsum('bqhmd,bkhmd->bhmqk', q, k).astype(jnp.float32) * (DIFF_HD ** -0.5)
    if mask is not None:
        s = jnp.where(mask, s, -jnp.inf)
    p = jax.nn.softmax(s, axis=-1)
    p = p[:, :, 0] - lam * p[:, :, 1]
    return jnp.einsum('bhqk,bkhe->bqhe', p.astype(v.dtype), v)


def prompt_attend(q, k, v, lam):
    B, S = q.shape[0], q.shape[1]
    nb = S // Q_BLOCK
    qb = jnp.moveaxis(q.reshape(B, nb, Q_BLOCK, DIFF_HEADS, 2, DIFF_HD), 1, 0)
    k_chunk = jnp.arange(S) // CHUNK

    def blk(args):
        q_i, i = args
        q_chunk = (i * Q_BLOCK + jnp.arange(Q_BLOCK)) // CHUNK
        mask = k_chunk[None, :] <= q_chunk[:, None]
        return diff_core(q_i, k, v, lam, mask)

    ob = lax.map(blk, (qb, jnp.arange(nb)))
    return jnp.moveaxis(ob, 0, 1).reshape(B, S, DIFF_HEADS, 2 * DIFF_HD)


def diff_out(o, subln_gain, lam_init):
    B, T = o.shape[:2]
    return (rms_norm(o, subln_gain) * (1 - lam_init)).reshape(B, T, D_DIFF)


def conv_ffn(xn, conv_prev, w_up, cw, cb, w_down):
    up = xn @ w_up
    gt, u = up[..., :D_FF], up[..., D_FF:]
    T = gt.shape[1]
    padded = jnp.concatenate([conv_prev.astype(gt.dtype), gt], axis=1)
    gc = sum((padded[:, j:j + T] * cw[j] for j in range(CONV_W)), cb)
    h = jax.nn.silu(gc) * u
    return h @ w_down, padded[:, T:]


def setup_inputs(seed: int = 0) -> dict:
    key = jax.random.key(seed)
    ks = jax.random.split(key, 32)
    it = (ks[i] for i in range(32))
    nrm = lambda shape, scale: jax.random.normal(next(it), shape, jnp.float32) * scale
    L = DEPTH
    d = {}
    d['x_prompt'] = nrm((BATCH, SEQ, D_MODEL), 1.0)
    d['x_sample'] = nrm((DEC_BATCH, DEC_SEQ, D_MODEL), 1.0)
    d['cache_k'] = nrm((L, DEC_BATCH, PAST_LEN, DIFF_HEADS, 2 * DIFF_HD), 1.0)
    d['cache_v'] = nrm((L, DEC_BATCH, PAST_LEN, DIFF_HEADS, 2 * DIFF_HD), 1.0)
    d['state_wkv'] = nrm((L, DEC_BATCH, RWKV_HEADS, RWKV_HEAD, RWKV_HEAD), 0.3)
    d['state_shift'] = nrm((L, DEC_BATCH, N_SHIFT), 1.0)
    d['state_conv'] = nrm((L, DEC_BATCH, CONV_W - 1, D_FF), 1.0)
    d['g_mix'] = 1.0 + nrm((L, D_MODEL), 0.05)
    d['w_in'] = nrm((L, D_MODEL, N_IN), D_MODEL ** -0.5)
    d['mu_shift'] = jax.random.uniform(next(it), (L, N_SHIFT), jnp.float32)
    d['w0'] = nrm((L, D_RWKV), 0.5)
    d['w_decay'] = nrm((L, DECAY_LORA, D_RWKV), 0.1)
    d['a0'] = nrm((L, D_RWKV), 0.1)
    d['w_aaa'] = nrm((L, AAA_LORA, D_RWKV), 0.1)
    d['w_gate'] = nrm((L, GATE_LORA, D_RWKV), GATE_LORA ** -0.5)
    d['k_k'] = 1.0 + nrm((L, D_RWKV), 0.1)
    d['k_a'] = 1.0 + nrm((L, D_RWKV), 0.1)
    d['r_k'] = nrm((L, RWKV_HEADS, RWKV_HEAD), 0.1)
    d['gn_w'] = 1.0 + nrm((L, D_RWKV), 0.05)
    d['gn_b'] = nrm((L, D_RWKV), 0.02)
    d['q_gain'] = 1.0 + nrm((L, DIFF_HD), 0.05)
    d['k_gain'] = 1.0 + nrm((L, DIFF_HD), 0.05)
    d['lambdas'] = nrm((L, 4, DIFF_HD), 0.1)
    d['subln_gain'] = 1.0 + nrm((L, 2 * DIFF_HD), 0.05)
    d['w_out'] = nrm((L, D_MIX, D_MODEL), D_MIX ** -0.5)
    d['g_ffn'] = 1.0 + nrm((L, D_MODEL), 0.05)
    d['w_ffn_in'] = nrm((L, D_MODEL, 2 * D_FF), D_MODEL ** -0.5)
    d['conv_w'] = nrm((L, CONV_W, D_FF), CONV_W ** -0.5)
    d['conv_b'] = nrm((L, D_FF), 0.02)
    d['w_ffn_out'] = nrm((L, D_FF, D_MODEL), D_FF ** -0.5)
    return d


def reference(x_prompt, x_sample, cache_k, cache_v, state_wkv, state_shift, state_conv,
              g_mix, w_in, mu_shift, w0, w_decay, a0, w_aaa, w_gate, k_k, k_a, r_k, gn_w, gn_b,
              q_gain, k_gain, lambdas, subln_gain, w_out, g_ffn, w_ffn_in, conv_w, conv_b, w_ffn_out):

    def layer(h, l, shift_prev, wkv_prev, conv_prev, attend):
        xn = rms_norm(h, g_mix[l])
        p = xn @ w_in[l]
        y_r, shift_new, wkv_new = rwkv_time_mix(p[..., :N_SHIFT], shift_prev, wkv_prev, mu_shift[l], w0[l],
                                                w_decay[l], a0[l], w_aaa[l], w_gate[l], k_k[l], k_a[l],
                                                r_k[l], gn_w[l], gn_b[l])
        q, k, v = diff_qkv(p[..., N_SHIFT:], q_gain[l], k_gain[l])
        lam_init = lambda_init(l)
        lam = diff_lambda(lambdas[l], lam_init)
        y_a = diff_out(attend(q, k, v, lam, l), subln_gain[l], lam_init)
        h = h + jnp.concatenate([y_r, y_a], axis=-1) @ w_out[l]
        f, conv_new = conv_ffn(rms_norm(h, g_ffn[l]), conv_prev, w_ffn_in[l], conv_w[l], conv_b[l], w_ffn_out[l])
        h = h + f
        B, T = k.shape[0], k.shape[1]
        return h, k.reshape(B, T, DIFF_HEADS, 2 * DIFF_HD), v, wkv_new, shift_new, conv_new

    def attend_prompt(q, k, v, lam, l):
        return prompt_attend(q, k, v, lam)

    def attend_sample(q, k, v, lam, l):
        B = q.shape[0]
        kc = cache_k[l].reshape(B, -1, DIFF_HEADS, 2, DIFF_HD).astype(k.dtype)
        k_all = jnp.concatenate([kc, k], axis=1)
        v_all = jnp.concatenate([cache_v[l].astype(v.dtype), v], axis=1)
        return diff_core(q, k_all, v_all, lam, None)

    Bp = x_prompt.shape[0]
    h = x_prompt
    pk, pv, pw, ps, pc = [], [], [], [], []
    for l in range(DEPTH):
        h, k_l, v_l, w_l, s_l, c_l = layer(h, l, jnp.zeros((Bp, N_SHIFT), x_prompt.dtype),
                                           jnp.zeros((Bp, RWKV_HEADS, RWKV_HEAD, RWKV_HEAD), jnp.float32),
                                           jnp.zeros((Bp, CONV_W - 1, D_FF), x_prompt.dtype), attend_prompt)
        pk.append(k_l); pv.append(v_l); pw.append(w_l); ps.append(s_l); pc.append(c_l)
    y_prompt = h

    h = x_sample
    sk, sv, sw, ss, sc = [], [], [], [], []
    for l in range(DEPTH):
        h, k_l, v_l, w_l, s_l, c_l = layer(h, l, state_shift[l], state_wkv[l], state_conv[l], attend_sample)
        sk.append(k_l); sv.append(v_l); sw.append(w_l); ss.append(s_l); sc.append(c_l)
    y_sample = h

    return (y_prompt, y_sample,
            jnp.stack(pk), jnp.stack(pv), jnp.stack(pw), jnp.stack(ps), jnp.stack(pc),
            jnp.stack(sk), jnp.stack(sv), jnp.stack(sw), jnp.stack(ss), jnp.stack(sc))
```

```python
import functools
import math

import jax
import jax.numpy as jnp
from jax import lax
from jax.experimental import pallas as pl
from jax.experimental.pallas import tpu as pltpu

F32 = jnp.float32
BF16 = jnp.bfloat16

HEAD = 64
PAIR = 2 * HEAD
CHUNK = 64
NORM_EPS = 1e-6
GN_EPS = 64e-5
NEG_BIG = -1e30
VMEM_LIMIT = 56 * 1024 * 1024

NN = (((1,), (0,)), ((), ()))
NT = (((1,), (1,)), ((), ()))
TN = (((0,), (0,)), ((), ()))


def _dg(a, b, dn):
    return lax.dot_general(a, b, dn, preferred_element_type=F32)


def _split2(x):
    hi = x.astype(BF16)
    lo = (x - hi.astype(F32)).astype(BF16)
    return hi, lo


def _split3(x):
    hi = x.astype(BF16)
    r = x - hi.astype(F32)
    mid = r.astype(BF16)
    lo = (r - mid.astype(F32)).astype(BF16)
    return hi, mid, lo


def _mm(a, b, dn=NN, passes=3):
    if passes == 1:
        return _dg(a.astype(BF16), b.astype(BF16), dn)
    ah, al = _split2(a)
    bh, bl = _split2(b)
    return _dg(ah, bh, dn) + (_dg(ah, bl, dn) + _dg(al, bh, dn))


def _mm_exact_rhs(a, e):
    hi, mid, lo = _split3(a)
    return _dg(hi, e, NN) + (_dg(mid, e, NN) + _dg(lo, e, NN))


def _mm_exact_lhs(e, b):
    hi, mid, lo = _split3(b)
    return _dg(e, hi, NN) + (_dg(e, mid, NN) + _dg(e, lo, NN))


def _segsum(x, ones_bd):
    w = ones_bd.shape[0]
    parts = [_mm_exact_rhs(x[:, i:i + w], ones_bd) for i in range(0, x.shape[1], w)]
    return parts[0] if len(parts) == 1 else jnp.concatenate(parts, axis=1)


def _params(*sem):
    return pltpu.CompilerParams(dimension_semantics=sem, vmem_limit_bytes=VMEM_LIMIT)


def _const_spec(shape, grid_rank):
    zeros = (0,) * len(shape)
    if grid_rank == 1:
        return pl.BlockSpec(shape, lambda i: zeros, pipeline_mode=pl.Buffered(1))
    if grid_rank == 2:
        return pl.BlockSpec(shape, lambda i, j: zeros, pipeline_mode=pl.Buffered(1))
    return pl.BlockSpec(shape, lambda i, j, k: zeros, pipeline_mode=pl.Buffered(1))


def _inproj_body(x_ref, g_ref, w_ref, qg_ref, kg_ref, ones_ref,
                 ps_ref, q_ref, k_ref, v_ref, kb_ref, vb_ref, *, ns, dd):
    x = x_ref[...]
    ms = jnp.mean(x * x, axis=-1, keepdims=True)
    xn = (x * lax.rsqrt(ms + NORM_EPS) * g_ref[...]).astype(BF16)
    ps_ref[...] = _dg(xn, w_ref[:, :ns], NN)
    ones = ones_ref[...]

    def head_norm(t, gain):
        ss = _segsum(t * t, ones)
        return t * lax.rsqrt(ss * (1.0 / HEAD) + NORM_EPS) * gain

    q = head_norm(_dg(xn, w_ref[:, ns:ns + dd], NN), qg_ref[...])
    q_ref[...] = (q * (HEAD ** -0.5)).astype(BF16)
    k = head_norm(_dg(xn, w_ref[:, ns + dd:ns + 2 * dd], NN), kg_ref[...])
    k_ref[...] = k
    kb_ref[...] = k.astype(BF16)
    v = _dg(xn, w_ref[:, ns + 2 * dd:ns + 3 * dd], NN)
    v_ref[...] = v
    vb_ref[...] = v.astype(BF16)


def _inproj(h, g, w_bf, q_gain, k_gain, ones_pair, ns, dd):
    n, d = h.shape
    tm = min(512, n)
    row = lambda w: pl.BlockSpec((tm, w), lambda i: (i, 0))
    return pl.pallas_call(
        functools.partial(_inproj_body, ns=ns, dd=dd),
        grid=(n // tm,),
        in_specs=[row(d), _const_spec((1, d), 1), _const_spec(w_bf.shape, 1),
                  _const_spec((1, dd), 1), _const_spec((1, dd), 1), _const_spec(ones_pair.shape, 1)],
        out_specs=[row(ns), row(dd), row(dd), row(dd), row(dd), row(dd)],
        out_shape=[jax.ShapeDtypeStruct((n, ns), F32), jax.ShapeDtypeStruct((n, dd), BF16),
                   jax.ShapeDtypeStruct((n, dd), F32), jax.ShapeDtypeStruct((n, dd), F32),
                   jax.ShapeDtypeStruct((n, dd), BF16), jax.ShapeDtypeStruct((n, dd), BF16)],
        compiler_params=_params("arbitrary"),
        name="inproj",
    )(h, g, w_bf, q_gain, k_gain, ones_pair)


def _bd2(x, m0, swap=False):
    zero = jnp.zeros_like(x)
    first = jnp.where(m0, x, zero)
    second = jnp.where(m0, zero, x)
    return jnp.concatenate([second, first] if swap else [first, second], axis=0)


def _wkv_prep_body(ps_ref, prev_ref, sp_ref, mu_ref, w0_ref, a0_ref, kk_ref, ka_ref, rk_ref,
                   wd_ref, wa_ref, wg_ref, ones_ref, tril_ref,
                   rp_ref, y0_ref, pc_ref, hinc_ref, bonus_ref, g_ref, *, dr, nd, na, passes):
    c = pl.program_id(1)
    x = ps_ref[...]
    row0 = jnp.where(c == 0, sp_ref[...], prev_ref[7:8, :])
    rowi = lax.broadcasted_iota(jnp.int32, x.shape, 0)
    prev = jnp.where(rowi == 0, row0, pltpu.roll(x, 1, axis=0))
    xs = x + (prev - x) * mu_ref[...]
    r = xs[:, :dr]
    k = xs[:, dr:2 * dr]
    v = xs[:, 2 * dr:3 * dr]
    o = 3 * dr
    xw = xs[:, o:o + nd]
    xa = xs[:, o + nd:o + nd + na]
    xg = xs[:, o + nd + na:]

    z = w0_ref[...] + _mm(jnp.tanh(xw), wd_ref[...], NN, 3)
    softplus = jnp.maximum(-z, 0.0) + jnp.log1p(jnp.exp(-jnp.abs(z)))
    ld = -jnp.exp(-softplus - 0.5)
    a = jax.nn.sigmoid(a0_ref[...] + _mm(xa, wa_ref[...], NN, 3))
    g_ref[...] = _mm(jax.nn.sigmoid(xg), wg_ref[...], NN, 1)

    ones = ones_ref[...]
    kk = k * kk_ref[...]
    kk = kk / jnp.maximum(jnp.sqrt(_segsum(kk * kk, ones)), 1e-12)
    k2 = k * (1.0 + (a - 1.0) * ka_ref[...])
    bonus_ref[...] = _segsum(r * k2 * rk_ref[...], ones) * v
    av = -kk
    bv = kk * a

    cum = _mm_exact_lhs(tril_ref[...], ld)
    cum_c = cum[CHUNK - 1:CHUNK, :]
    e_neg = jnp.exp(-cum)
    e_end = jnp.exp(cum_c - cum)
    at = av * jnp.exp(cum - ld)
    rt = r * jnp.exp(cum)
    bt = bv * e_neg
    kt = k2 * e_neg
    bh = bv * e_end
    kh = k2 * e_end
    g_c = jnp.exp(cum_c)

    lane = lax.broadcasted_iota(jnp.int32, (CHUNK, PAIR), 1)
    rr = lax.broadcasted_iota(jnp.int32, (CHUNK, PAIR), 0)
    m0 = lane < HEAD
    lane_in = jnp.where(m0, lane, lane - HEAD)
    strict = rr > lane_in
    incl = rr >= lane_in
    diag = rr == lane_in
    eye2 = jnp.where(diag, 1.0, 0.0).astype(F32)
    m0w = lax.broadcasted_iota(jnp.int32, (2 * CHUNK, PAIR), 1) < HEAD
    zero = jnp.zeros((CHUNK, PAIR), F32)
    zero2 = jnp.zeros((2 * CHUNK, PAIR), F32)

    for j in range(dr // PAIR):
        sl = slice(j * PAIR, (j + 1) * PAIR)
        atp, rtp, btp, ktp, bhp, khp, vp = at[:, sl], rt[:, sl], bt[:, sl], kt[:, sl], bh[:, sl], kh[:, sl], v[:, sl]
        lhs = jnp.concatenate([atp, rtp], axis=0)
        l0 = jnp.where(m0w, lhs, zero2)
        l1 = jnp.where(m0w, zero2, lhs)
        g0 = _mm(l0, jnp.concatenate([btp, ktp], axis=0), NT, passes)
        g1 = _mm(l1, jnp.concatenate([ktp, btp], axis=0), NT, passes)
        t0, b0, t1, b1 = g0[:CHUNK], g0[CHUNK:], g1[:CHUNK], g1[CHUNK:]
        q = jnp.where(strict, jnp.where(m0, t0, t1), zero)
        ak = jnp.where(strict, jnp.where(m0, t1, t0), zero)
        mb = jnp.where(incl, jnp.where(m0, b0, b1), zero)
        mk = jnp.where(incl, jnp.where(m0, b1, b0), zero)

        tm = eye2 + q
        qp = q
        for _ in range(5):
            qp = _mm(qp, _bd2(qp, m0), NN, passes)
            tm = tm + _mm(tm, _bd2(qp, m0), NN, passes)

        kv = _mm(jnp.concatenate([ak, mk], axis=0), _bd2(vp, m0, swap=True), NN, passes)
        akv, mkv = kv[:CHUNK], kv[CHUNK:]
        au = _mm(tm, jnp.concatenate([_bd2(atp, m0), _bd2(akv, m0)], axis=1), NN, passes)
        ap, u0 = au[:, :PAIR], au[:, PAIR:]
        ry = _mm(mb, jnp.concatenate([_bd2(ap, m0), _bd2(u0, m0)], axis=1), NN, passes)
        rp_ref[:, sl] = rtp + ry[:, :PAIR]
        y0_ref[:, sl] = mkv + ry[:, PAIR:]

        rhs = jnp.concatenate([jnp.concatenate([ap, u0], axis=1),
                               jnp.concatenate([zero, vp], axis=1)], axis=0)
        ph = _mm(jnp.concatenate([bhp, khp], axis=0), rhs, TN, passes)
        pc_ref[:, sl] = (jnp.where(m0, ph[:CHUNK, :PAIR], ph[CHUNK:, :PAIR])
                         + jnp.where(diag, g_c[:, sl], 0.0))
        hinc_ref[:, sl] = jnp.where(m0, ph[:CHUNK, PAIR:], ph[CHUNK:, PAIR:])


def _wkv_prep(ps, shift_prev, lp, ones_pair, tril, batch, seq, passes):
    n, ns = ps.shape
    dr = lp["w0"].shape[-1]
    nd = lp["w_decay"].shape[0]
    na = lp["w_aaa"].shape[0]
    nc = seq // CHUNK
    blk = CHUNK // 8
    row = lambda w: pl.BlockSpec((CHUNK, w), lambda b, c: (b * nc + c, 0))
    cs = lambda a: _const_spec(a.shape, 2)
    out = jax.ShapeDtypeStruct((n, dr), F32)
    consts = [lp["mu"], lp["w0"], lp["a0"], lp["k_k"], lp["k_a"], lp["r_k"],
              lp["w_decay"], lp["w_aaa"], lp["w_gate"], ones_pair, tril]
    return pl.pallas_call(
        functools.partial(_wkv_prep_body, dr=dr, nd=nd, na=na, passes=passes),
        grid=(batch, nc),
        in_specs=[row(ns),
                  pl.BlockSpec((8, ns), lambda b, c: (jnp.maximum((b * nc + c) * blk - 1, 0), 0)),
                  pl.BlockSpec((None, 1, ns), lambda b, c: (b, 0, 0))] + [cs(a) for a in consts],
        out_specs=[row(dr)] * 6,
        out_shape=[out] * 6,
        compiler_params=_params("arbitrary", "arbitrary"),
        name="wkv_prep",
    )(ps, ps, shift_prev, *consts)


def _wkv_scan_body(rp_ref, y0_ref, pc_ref, hinc_ref, bonus_ref, g_ref, h0_ref, gnw_ref, gnb_ref, ones_ref,
                   y_ref, hout_ref, h_scr, *, dr, passes):
    c = pl.program_id(1)

    @pl.when(c == 0)
    def _():
        h_scr[...] = h0_ref[...]

    m0 = lax.broadcasted_iota(jnp.int32, (CHUNK, PAIR), 1) < HEAD
    ys = []
    for j in range(dr // PAIR):
        sl = slice(j * PAIR, (j + 1) * PAIR)
        lhs = jnp.concatenate([rp_ref[:, sl], pc_ref[:, sl]], axis=0)
        out = _mm(lhs, _bd2(h_scr[:, sl], m0), NN, passes)
        ys.append(y0_ref[:, sl] + out[:CHUNK])
        h_scr[:, sl] = hinc_ref[:, sl] + out[CHUNK:]
    hout_ref[...] = h_scr[...]

    y = jnp.concatenate(ys, axis=1)
    ones = ones_ref[...]
    mean = _segsum(y, ones) * (1.0 / HEAD)
    d = y - mean
    var = _segsum(d * d, ones) * (1.0 / HEAD)
    yn = d * lax.rsqrt(var + GN_EPS) * gnw_ref[...] + gnb_ref[...]
    y_ref[...] = ((yn + bonus_ref[...]) * g_ref[...]).astype(BF16)


def _wkv_scan(rp, y0, pc, hinc, bonus, g, h0, gn_w, gn_b, ones_pair, batch, seq, passes):
    n, dr = rp.shape
    nc = seq // CHUNK
    row = pl.BlockSpec((CHUNK, dr), lambda b, c: (b * nc + c, 0))
    st = pl.BlockSpec((None, HEAD, dr), lambda b, c: (b, 0, 0))
    return pl.pallas_call(
        functools.partial(_wkv_scan_body, dr=dr, passes=passes),
        grid=(batch, nc),
        in_specs=[row] * 6 + [st, _const_spec((1, dr), 2), _const_spec((1, dr), 2),
                              _const_spec(ones_pair.shape, 2)],
        out_specs=[row, st],
        out_shape=[jax.ShapeDtypeStruct((n, dr), BF16), jax.ShapeDtypeStruct((batch, HEAD, dr), F32)],
        scratch_shapes=[pltpu.VMEM((HEAD, dr), F32)],
        compiler_params=_params("arbitrary", "arbitrary"),
        name="wkv_scan",
    )(rp, y0, pc, hinc, bonus, g, h0, gn_w, gn_b, ones_pair)


def _flash_init(m_scr, l_scr, acc_scr):
    m_scr[...] = jnp.full(m_scr.shape, NEG_BIG, F32)
    l_scr[...] = jnp.zeros(l_scr.shape, F32)
    acc_scr[...] = jnp.zeros(acc_scr.shape, F32)


def _flash_update(q, k, v, mask, m_scr, l_scr, acc_scr):
    tq = q.shape[0]
    first = lax.broadcasted_iota(jnp.int32, (tq, PAIR), 1) < HEAD
    zq = jnp.zeros((tq, PAIR), BF16)
    for h in range(q.shape[1] // PAIR):
        sl = slice(h * PAIR, (h + 1) * PAIR)
        qh, kh, vh = q[:, sl], k[:, sl], v[:, sl]
        for m in range(2):
            qm = jnp.where(first, qh, zq) if m == 0 else jnp.where(first, zq, qh)
            s = _dg(qm, kh, NT)
            if mask is not None:
                s = jnp.where(mask, s, NEG_BIG)
            i = 2 * h + m
            m_prev = m_scr[i]
            m_new = jnp.maximum(m_prev, jnp.max(s, axis=1, keepdims=True))
            alpha = jnp.exp(m_prev - m_new)
            p = jnp.exp(s - m_new[:, :1])
            l_scr[i] = alpha * l_scr[i] + jnp.sum(p, axis=1, keepdims=True)
            acc_scr[i] = alpha * acc_scr[i] + _dg(p.astype(BF16), vh, NN)
            m_scr[i] = m_new


def _flash_finish(lam, sub_ref, scale, y_ref, l_scr, acc_scr):
    for h in range(y_ref.shape[1] // PAIR):
        o = acc_scr[2 * h] / l_scr[2 * h] - lam * (acc_scr[2 * h + 1] / l_scr[2 * h + 1])
        ms = jnp.mean(o * o, axis=-1, keepdims=True)
        y = o * lax.rsqrt(ms + NORM_EPS) * sub_ref[...] * scale
        y_ref[:, h * PAIR:(h + 1) * PAIR] = y.astype(BF16)


def _attn_prompt_body(lam_ref, q_ref, k_ref, v_ref, sub_ref, y_ref, m_scr, l_scr, acc_scr, *, tq, tk, scale):
    qi = pl.program_id(1)
    kj = pl.program_id(2)

    @pl.when(kj == 0)
    def _():
        _flash_init(m_scr, l_scr, acc_scr)

    @pl.when(kj * tk < (qi + 1) * tq)
    def _():
        qrow = qi * tq + lax.broadcasted_iota(jnp.int32, (tq, tk), 0)
        kcol = kj * tk + lax.broadcasted_iota(jnp.int32, (tq, tk), 1)
        mask = (kcol // CHUNK) <= (qrow // CHUNK)
        _flash_update(q_ref[...], k_ref[...], v_ref[...], mask, m_scr, l_scr, acc_scr)

    @pl.when(kj == pl.num_programs(2) - 1)
    def _():
        _flash_finish(lam_ref[0], sub_ref, scale, y_ref, l_scr, acc_scr)


def _attn_prompt(lam, q, kb, vb, subln, scale, batch, seq):
    n, dd = q.shape
    tq = min(512, seq)
    tk = tq
    nq, nk = seq // tq, seq // tk
    nh2 = 2 * dd // PAIR
    kv_spec = pl.BlockSpec((tk, dd), lambda b, i, j: (b * nk + jnp.minimum(j, i), 0))
    q_spec = pl.BlockSpec((tq, dd), lambda b, i, j: (b * nq + i, 0))
    return pl.pallas_call(
        functools.partial(_attn_prompt_body, tq=tq, tk=tk, scale=scale),
        grid=(batch, nq, nk),
        in_specs=[pl.BlockSpec(memory_space=pltpu.SMEM), q_spec, kv_spec, kv_spec, _const_spec((1, PAIR), 3)],
        out_specs=q_spec,
        out_shape=jax.ShapeDtypeStruct((n, dd), BF16),
        scratch_shapes=[pltpu.VMEM((nh2, tq, PAIR), F32)] * 3,
        compiler_params=_params("arbitrary", "arbitrary", "arbitrary"),
        name="attn_prompt",
    )(lam, q, kb, vb, subln)


def _attn_sample_body(lam_ref, q_ref, ck_ref, cv_ref, k_ref, v_ref, sub_ref, y_ref, m_scr, l_scr, acc_scr, *, scale):
    kj = pl.program_id(1)
    last = pl.num_programs(1) - 1

    @pl.when(kj == 0)
    def _():
        _flash_init(m_scr, l_scr, acc_scr)

    @pl.when(kj < last)
    def _():
        _flash_update(q_ref[...], ck_ref[...].astype(BF16), cv_ref[...].astype(BF16), None, m_scr, l_scr, acc_scr)

    @pl.when(kj == last)
    def _():
        _flash_update(q_ref[...], k_ref[...], v_ref[...], None, m_scr, l_scr, acc_scr)
        _flash_finish(lam_ref[0], sub_ref, scale, y_ref, l_scr, acc_scr)


def _attn_sample(lam, q, cache_k, cache_v, layer, kb, vb, subln, scale, batch, seq):
    n, dd = q.shape
    past = cache_k.shape[2]
    tk = min(1024, past)
    nk = past // tk
    nh2 = 2 * dd // PAIR
    row = pl.BlockSpec((seq, dd), lambda b, j: (b, 0))
    cache = pl.BlockSpec((None, None, tk, dd), lambda b, j: (layer, b, jnp.minimum(j, nk - 1), 0))
    return pl.pallas_call(
        functools.partial(_attn_sample_body, scale=scale),
        grid=(batch, nk + 1),
        in_specs=[pl.BlockSpec(memory_space=pltpu.SMEM), row, cache, cache, row, row, _const_spec((1, PAIR), 2)],
        out_specs=row,
        out_shape=jax.ShapeDtypeStruct((n, dd), BF16),
        scratch_shapes=[pltpu.VMEM((nh2, seq, PAIR), F32)] * 3,
        compiler_params=_params("arbitrary", "arbitrary"),
        name="attn_sample",
    )(lam, q, cache_k, cache_v, kb, vb, subln)


def _ffn_body(h_ref, yr_ref, ya_ref, wo_ref, gf_ref, wup_ref, cw_ref, cb_ref, wdn_ref, cprev_ref,
              hout_ref, cnew_ref, carry_scr, *, dff, fc):
    t = pl.program_id(1)

    @pl.when(t == 0)
    def _():
        carry_scr[...] = cprev_ref[...]

    y = jnp.concatenate([yr_ref[...], ya_ref[...]], axis=1)
    h1 = h_ref[...] + _dg(y, wo_ref[...], NN)
    ms = jnp.mean(h1 * h1, axis=-1, keepdims=True)
    xn = (h1 * lax.rsqrt(ms + NORM_EPS) * gf_ref[...]).astype(BF16)
    tm = h1.shape[0]
    rowi = lax.broadcasted_iota(jnp.int32, (tm, fc), 0)
    acc = jnp.zeros(h1.shape, F32)
    for c in range(dff // fc):
        sl = slice(c * fc, (c + 1) * fc)
        gt = _dg(xn, wup_ref[:, sl], NN)
        u = _dg(xn, wup_ref[:, dff + c * fc:dff + (c + 1) * fc], NN)
        p1 = carry_scr[7:8, sl]
        p2 = carry_scr[6:7, sl]
        g1 = jnp.where(rowi == 0, p1, pltpu.roll(gt, 1, axis=0))
        g2 = jnp.where(rowi == 0, p2, jnp.where(rowi == 1, p1, pltpu.roll(gt, 2, axis=0)))
        gc = cb_ref[:, sl] + g2 * cw_ref[0:1, sl] + g1 * cw_ref[1:2, sl] + gt * cw_ref[2:3, sl]
        carry_scr[:, sl] = gt[tm - 8:, :]
        hh = (gc * jax.nn.sigmoid(gc) * u).astype(BF16)
        acc = acc + _dg(hh, wdn_ref[sl, :], NN)
    hout_ref[...] = h1 + acc
    cnew_ref[...] = carry_scr[...]


def _ffn(h, yr, ya, wo, g_ffn, wup, cw, cb, wdn, conv_prev8, batch, seq):
    n, d = h.shape
    dff = cb.shape[-1]
    fc = 256
    tm = min(512, seq)
    nt = seq // tm
    row = lambda w: pl.BlockSpec((tm, w), lambda b, t: (b * nt + t, 0))
    st = pl.BlockSpec((None, 8, dff), lambda b, t: (b, 0, 0))
    cs = lambda a: _const_spec(a.shape, 2)
    return pl.pallas_call(
        functools.partial(_ffn_body, dff=dff, fc=fc),
        grid=(batch, nt),
        in_specs=[row(d), row(yr.shape[1]), row(ya.shape[1]), cs(wo), cs(g_ffn), cs(wup), cs(cw), cs(cb), cs(wdn), st],
        out_specs=[row(d), st],
        out_shape=[jax.ShapeDtypeStruct((n, d), F32), jax.ShapeDtypeStruct((batch, 8, dff), F32)],
        scratch_shapes=[pltpu.VMEM((8, dff), F32)],
        compiler_params=_params("arbitrary", "arbitrary"),
        name="ffn",
    )(h, yr, ya, wo, g_ffn, wup, cw, cb, wdn, conv_prev8)


WKV_PASSES = 3


def _lambda_init(layer):
    return 0.8 - 0.6 * math.exp(-0.3 * layer)


def _run_group(x, depth, layers, consts, shift_prev, wkv_prev, conv_prev, attend):
    batch, seq, d = x.shape
    h = x.reshape(batch * seq, d)
    ones_pair, tril = consts
    ks, vs, ws, ss, cs = [], [], [], [], []
    for l in range(depth):
        lp = layers[l]
        ns = lp["mu"].shape[-1]
        dd = lp["q_gain"].shape[-1]
        dr = lp["w0"].shape[-1]
        dff = lp["conv_b"].shape[-1]
        ps, q, k, v, kb, vb = _inproj(h, lp["g_mix"], lp["w_in"], lp["q_gain"], lp["k_gain"], ones_pair, ns, dd)

        sp = jnp.zeros((batch, 1, ns), F32) if shift_prev is None else shift_prev[l][:, None, :]
        if wkv_prev is None:
            h0 = jnp.zeros((batch, HEAD, dr), F32)
        else:
            h0 = jnp.transpose(wkv_prev[l], (0, 3, 1, 2)).reshape(batch, HEAD, dr)
        rp, y0, pc, hinc, bonus, g = _wkv_prep(ps, sp, lp, ones_pair, tril, batch, seq, WKV_PASSES)
        yr, hout = _wkv_scan(rp, y0, pc, hinc, bonus, g, h0, lp["gn_w"], lp["gn_b"], ones_pair, batch, seq,
                             WKV_PASSES)

        lam_init = _lambda_init(l)
        ya = attend(l, lp["lam"], q, kb, vb, lp["subln"], 1.0 - lam_init, batch, seq)

        cp = jnp.zeros((batch, 8, dff), F32) if conv_prev is None else jnp.pad(conv_prev[l], ((0, 0), (6, 0), (0, 0)))
        h, cnew = _ffn(h, yr, ya, lp["w_out"], lp["g_ffn"], lp["w_up"], lp["conv_w"], lp["conv_b"], lp["w_down"],
                       cp, batch, seq)

        ks.append(k.reshape(batch, seq, dd // PAIR, PAIR))
        vs.append(v.reshape(batch, seq, dd // PAIR, PAIR))
        ws.append(jnp.transpose(hout.reshape(batch, HEAD, dr // HEAD, HEAD), (0, 2, 3, 1)))
        ss.append(ps.reshape(batch, seq, ns)[:, -1])
        cs.append(cnew[:, 6:8])
    return (h.reshape(batch, seq, d), jnp.stack(ks), jnp.stack(vs), jnp.stack(ws), jnp.stack(ss), jnp.stack(cs))


def kernel(x_prompt, x_sample, cache_k, cache_v, state_wkv, state_shift, state_conv, g_mix, w_in, mu_shift, w0, w_decay, a0, w_aaa, w_gate, k_k, k_a, r_k, gn_w, gn_b, q_gain, k_gain, lambdas, subln_gain, w_out, g_ffn, w_ffn_in, conv_w, conv_b, w_ffn_out):
    depth = w_in.shape[0]
    dr = w0.shape[-1]
    ns = mu_shift.shape[-1]
    dd = (w_in.shape[-1] - ns) // 3
    assert dr % PAIR == 0 and dd % PAIR == 0 and q_gain.shape[-1] == HEAD and subln_gain.shape[-1] == PAIR
    assert x_prompt.shape[1] % CHUNK == 0 and x_sample.shape[1] == CHUNK

    row = lambda a: a.reshape(1, -1).astype(F32)
    layers = []
    for l in range(depth):
        lv = lambdas[l].astype(F32)
        lam = jnp.exp(jnp.sum(lv[0] * lv[1])) - jnp.exp(jnp.sum(lv[2] * lv[3])) + _lambda_init(l)
        layers.append(dict(
            g_mix=row(g_mix[l]), w_in=w_in[l].astype(BF16), mu=row(mu_shift[l]), w0=row(w0[l]), a0=row(a0[l]),
            w_decay=w_decay[l], w_aaa=w_aaa[l], w_gate=w_gate[l], k_k=row(k_k[l]), k_a=row(k_a[l]), r_k=row(r_k[l]),
            gn_w=row(gn_w[l]), gn_b=row(gn_b[l]),
            q_gain=row(jnp.tile(q_gain[l], dd // HEAD)), k_gain=row(jnp.tile(k_gain[l], dd // HEAD)),
            lam=lam.reshape(1), subln=row(subln_gain[l]), w_out=w_out[l].astype(BF16), g_ffn=row(g_ffn[l]),
            w_up=w_ffn_in[l].astype(BF16), conv_w=conv_w[l], conv_b=row(conv_b[l]), w_down=w_ffn_out[l].astype(BF16)))

    ones_pair = jnp.kron(jnp.eye(2, dtype=F32), jnp.ones((HEAD, HEAD), F32)).astype(BF16)
    tril = jnp.tril(jnp.ones((CHUNK, CHUNK), F32)).astype(BF16)
    consts = (ones_pair, tril)

    def attend_prompt(l, lam, q, kb, vb, subln, scale, batch, seq):
        return _attn_prompt(lam, q, kb, vb, subln, scale, batch, seq)

    ck = cache_k.reshape(cache_k.shape[:3] + (dd,))
    cv = cache_v.reshape(cache_v.shape[:3] + (dd,))

    def attend_sample(l, lam, q, kb, vb, subln, scale, batch, seq):
        return _attn_sample(lam, q, ck, cv, l, kb, vb, subln, scale, batch, seq)

    yp, pk, pv, pw, ps_, pc = _run_group(x_prompt, depth, layers, consts, None, None, None, attend_prompt)
    ys, sk, sv, sw, ss, sc = _run_group(x_sample, depth, layers, consts, state_shift, state_wkv, state_conv,
                                        attend_sample)
    return (yp, ys, pk, pv, pw, ps_, pc, sk, sv, sw, ss, sc)
```

```python
import functools
import math

import jax
import jax.numpy as jnp
from jax import lax
from jax.experimental import pallas as pl
from jax.experimental.pallas import tpu as pltpu

F32 = jnp.float32
BF16 = jnp.bfloat16

HEAD = 64
PAIR = 2 * HEAD
CHUNK = 64
NORM_EPS = 1e-6
GN_EPS = 64e-5
NEG_BIG = -1e30
LOG2E = math.log2(math.e)
VMEM_LIMIT = 56 * 1024 * 1024

NN = (((1,), (0,)), ((), ()))
NT = (((1,), (1,)), ((), ()))
TN = (((0,), (0,)), ((), ()))


def _dg(a, b, dn):
    return lax.dot_general(a, b, dn, preferred_element_type=F32)


def _split2(x):
    hi = x.astype(BF16)
    lo = (x - hi.astype(F32)).astype(BF16)
    return hi, lo


def _split3(x):
    hi = x.astype(BF16)
    r = x - hi.astype(F32)
    mid = r.astype(BF16)
    lo = (r - mid.astype(F32)).astype(BF16)
    return hi, mid, lo


def _mm(a, b, dn=NN, passes=3):
    if passes == 1:
        return _dg(a.astype(BF16), b.astype(BF16), dn)
    ah, al = _split2(a)
    bh, bl = _split2(b)
    return _dg(ah, bh, dn) + (_dg(ah, bl, dn) + _dg(al, bh, dn))


def _mm_exact_rhs(a, e):
    hi, lo = _split2(a)
    return _dg(hi, e, NN) + _dg(lo, e, NN)


def _mm_exact_lhs(e, b):
    hi, lo = _split2(b)
    return _dg(e, hi, NN) + _dg(e, lo, NN)


def _segsum(x, ones_bd):
    w = ones_bd.shape[0]
    parts = [_mm_exact_rhs(x[:, i:i + w], ones_bd) for i in range(0, x.shape[1], w)]
    return parts[0] if len(parts) == 1 else jnp.concatenate(parts, axis=1)


def _params(*sem):
    return pltpu.CompilerParams(dimension_semantics=sem, vmem_limit_bytes=VMEM_LIMIT)


def _const_spec(shape, grid_rank):
    zeros = (0,) * len(shape)
    if grid_rank == 1:
        return pl.BlockSpec(shape, lambda i: zeros, pipeline_mode=pl.Buffered(1))
    if grid_rank == 2:
        return pl.BlockSpec(shape, lambda i, j: zeros, pipeline_mode=pl.Buffered(1))
    return pl.BlockSpec(shape, lambda i, j, k: zeros, pipeline_mode=pl.Buffered(1))


def _inproj_body(x_ref, g_ref, w_ref, qg_ref, kg_ref, ones_ref,
                 ps_ref, q_ref, k_ref, v_ref, kb_ref, vb_ref, *, ns, dd):
    x = x_ref[...]
    ms = jnp.mean(x * x, axis=-1, keepdims=True)
    xn = (x * lax.rsqrt(ms + NORM_EPS) * g_ref[...]).astype(BF16)
    ps_ref[...] = _dg(xn, w_ref[:, :ns], NN)
    ones = ones_ref[...]

    def head_norm(t, gain):
        ss = _segsum(t * t, ones)
        return t * lax.rsqrt(ss * (1.0 / HEAD) + NORM_EPS) * gain

    q = _dg(xn, w_ref[:, ns:ns + dd], NN)
    k = _dg(xn, w_ref[:, ns + dd:ns + 2 * dd], NN)
    v = _dg(xn, w_ref[:, ns + 2 * dd:ns + 3 * dd], NN)
    v_ref[...] = v
    vb_ref[...] = v.astype(BF16)
    q = head_norm(q, qg_ref[...])
    q_ref[...] = (q * (HEAD ** -0.5 * LOG2E)).astype(BF16)
    k = head_norm(k, kg_ref[...])
    k_ref[...] = k
    kb_ref[...] = k.astype(BF16)


def _inproj(h, g, w_bf, q_gain, k_gain, ones_pair, ns, dd):
    n, d = h.shape
    tm = min(512, n)
    row = lambda w: pl.BlockSpec((tm, w), lambda i: (i, 0))
    return pl.pallas_call(
        functools.partial(_inproj_body, ns=ns, dd=dd),
        grid=(n // tm,),
        in_specs=[row(d), _const_spec((1, d), 1), _const_spec(w_bf.shape, 1),
                  _const_spec((1, dd), 1), _const_spec((1, dd), 1), _const_spec(ones_pair.shape, 1)],
        out_specs=[row(ns), row(dd), row(dd), row(dd), row(dd), row(dd)],
        out_shape=[jax.ShapeDtypeStruct((n, ns), F32), jax.ShapeDtypeStruct((n, dd), BF16),
                   jax.ShapeDtypeStruct((n, dd), F32), jax.ShapeDtypeStruct((n, dd), F32),
                   jax.ShapeDtypeStruct((n, dd), BF16), jax.ShapeDtypeStruct((n, dd), BF16)],
        compiler_params=_params("arbitrary"),
        name="inproj",
    )(h, g, w_bf, q_gain, k_gain, ones_pair)


def _bd2(x, m0, swap=False):
    zero = jnp.zeros_like(x)
    first = jnp.where(m0, x, zero)
    second = jnp.where(m0, zero, x)
    return jnp.concatenate([second, first] if swap else [first, second], axis=0)


def _wkv_prep_body(ps_ref, prev_ref, sp_ref, mu_ref, w0_ref, a0_ref, kk_ref, ka_ref, rk_ref,
                   wd_ref, wa_ref, wg_ref, ones_ref, tril_ref,
                   rp_ref, y0_ref, pc_ref, hinc_ref, bonus_ref, g_ref, *, dr, nd, na, passes):
    c = pl.program_id(1)
    x = ps_ref[...]
    row0 = jnp.where(c == 0, sp_ref[...], prev_ref[7:8, :])
    rowi = lax.broadcasted_iota(jnp.int32, x.shape, 0)
    prev = jnp.where(rowi == 0, row0, pltpu.roll(x, 1, axis=0))
    xs = x + (prev - x) * mu_ref[...]
    r = xs[:, :dr]
    k = xs[:, dr:2 * dr]
    v = xs[:, 2 * dr:3 * dr]
    o = 3 * dr
    xw = xs[:, o:o + nd]
    xa = xs[:, o + nd:o + nd + na]
    xg = xs[:, o + nd + na:]

    z = w0_ref[...] + _mm(jnp.tanh(xw), wd_ref[...], NN, 3)
    softplus = jnp.maximum(-z, 0.0) + jnp.log1p(jnp.exp(-jnp.abs(z)))
    ld = -jnp.exp(-softplus - 0.5)
    a = jax.nn.sigmoid(a0_ref[...] + _mm(xa, wa_ref[...], NN, 3))
    g_ref[...] = _mm(jax.nn.sigmoid(xg), wg_ref[...], NN, 1)

    ones = ones_ref[...]
    kk = k * kk_ref[...]
    kk = kk / jnp.maximum(jnp.sqrt(_segsum(kk * kk, ones)), 1e-12)
    k2 = k * (1.0 + (a - 1.0) * ka_ref[...])
    bonus_ref[...] = _segsum(r * k2 * rk_ref[...], ones) * v
    av = -kk
    bv = kk * a

    cum = _mm_exact_lhs(tril_ref[...], ld)
    e_neg = jnp.exp(-cum)
    at_all = av * jnp.exp(cum - ld)
    rt_all = r * jnp.exp(cum)
    bt_all = bv * e_neg
    kt_all = k2 * e_neg

    lane = lax.broadcasted_iota(jnp.int32, (CHUNK, PAIR), 1)
    rr = lax.broadcasted_iota(jnp.int32, (CHUNK, PAIR), 0)
    m0 = lane < HEAD
    lane_in = jnp.where(m0, lane, lane - HEAD)
    strict = rr > lane_in
    incl = rr >= lane_in
    diag = rr == lane_in
    eye2 = jnp.where(diag, 1.0, 0.0).astype(F32)
    m0w = lax.broadcasted_iota(jnp.int32, (2 * CHUNK, PAIR), 1) < HEAD
    zero = jnp.zeros((CHUNK, PAIR), F32)
    zero2 = jnp.zeros((2 * CHUNK, PAIR), F32)

    cast = (lambda t: t.astype(BF16)) if passes == 1 else (lambda t: t)
    bd = lambda t, swap=False: cast(_bd2(t, m0, swap))

    chains = [(slice(ch * CHUNK, (ch + 1) * CHUNK), slice(j * PAIR, (j + 1) * PAIR), (ch + 1) * CHUNK - 1)
              for ch in range(x.shape[0] // CHUNK) for j in range(dr // PAIR)]
    atp = [at_all[rs, sl] for rs, sl, _ in chains]
    rtp = [rt_all[rs, sl] for rs, sl, _ in chains]
    vp = [v[rs, sl] for rs, sl, _ in chains]
    g0, g1 = [], []
    for i, (rs, sl, _) in enumerate(chains):
        btp, ktp = bt_all[rs, sl], kt_all[rs, sl]
        lhs = jnp.concatenate([atp[i], rtp[i]], axis=0)
        g0.append(_mm(jnp.where(m0w, lhs, zero2), jnp.concatenate([btp, ktp], axis=0), NT, passes))
        g1.append(_mm(jnp.where(m0w, zero2, lhs), jnp.concatenate([ktp, btp], axis=0), NT, passes))
    qp = [jnp.where(strict, jnp.where(m0, a[:CHUNK], b[:CHUNK]), zero) for a, b in zip(g0, g1)]
    akmk = [jnp.concatenate([jnp.where(strict, jnp.where(m0, b[:CHUNK], a[:CHUNK]), zero),
                             jnp.where(incl, jnp.where(m0, b[CHUNK:], a[CHUNK:]), zero)], axis=0)
            for a, b in zip(g0, g1)]
    mb = [jnp.where(incl, jnp.where(m0, a[CHUNK:], b[CHUNK:]), zero) for a, b in zip(g0, g1)]
    kv = [_mm(l, bd(t, True), NN, passes) for l, t in zip(akmk, vp)]

    tm = [eye2 + t for t in qp]
    bq = [bd(t) for t in qp]
    for _ in range(5):
        qp = [_mm(t, b, NN, passes) for t, b in zip(qp, bq)]
        bq = [bd(t) for t in qp]
        tm = [t + _mm(t, b, NN, passes) for t, b in zip(tm, bq)]

    au = [_mm(t, jnp.concatenate([bd(a), bd(k[:CHUNK])], axis=1), NN, passes) for t, a, k in zip(tm, atp, kv)]
    ry = [_mm(t, jnp.concatenate([bd(a[:, :PAIR]), bd(a[:, PAIR:])], axis=1), NN, passes) for t, a in zip(mb, au)]
    ph = []
    for i, (rs, sl, last) in enumerate(chains):
        e_end = jnp.exp(cum[last:last + 1, sl] - cum[rs, sl])
        rhs = jnp.concatenate([au[i], jnp.concatenate([zero, vp[i]], axis=1)], axis=0)
        ph.append(_mm(jnp.concatenate([bv[rs, sl] * e_end, k2[rs, sl] * e_end], axis=0), rhs, TN, passes))
    for i, (rs, sl, last) in enumerate(chains):
        rp_ref[rs, sl] = rtp[i] + ry[i][:, :PAIR]
        y0_ref[rs, sl] = kv[i][CHUNK:] + ry[i][:, PAIR:]
        g_c = jnp.exp(cum[last:last + 1, sl])
        pc_ref[rs, sl] = (jnp.where(m0, ph[i][:CHUNK, :PAIR], ph[i][CHUNK:, :PAIR])
                          + jnp.where(diag, g_c, 0.0))
        hinc_ref[rs, sl] = jnp.where(m0, ph[i][:CHUNK, PAIR:], ph[i][CHUNK:, PAIR:])


def _wkv_prep(ps, shift_prev, lp, ones_pair, batch, seq, passes):
    n, ns = ps.shape
    dr = lp["w0"].shape[-1]
    nd = lp["w_decay"].shape[0]
    na = lp["w_aaa"].shape[0]
    rows = min(PREP_ROWS, seq)
    nc = seq // rows
    blk = rows // 8
    tril = jnp.kron(jnp.eye(rows // CHUNK, dtype=F32), jnp.tril(jnp.ones((CHUNK, CHUNK), F32))).astype(BF16)
    row = lambda w: pl.BlockSpec((rows, w), lambda b, c: (b * nc + c, 0))
    cs = lambda a: _const_spec(a.shape, 2)
    out = jax.ShapeDtypeStruct((n, dr), F32)
    consts = [lp["mu"], lp["w0"], lp["a0"], lp["k_k"], lp["k_a"], lp["r_k"],
              lp["w_decay"], lp["w_aaa"], lp["w_gate"], ones_pair, tril]
    return pl.pallas_call(
        functools.partial(_wkv_prep_body, dr=dr, nd=nd, na=na, passes=passes),
        grid=(batch, nc),
        in_specs=[row(ns),
                  pl.BlockSpec((8, ns), lambda b, c: (jnp.maximum((b * nc + c) * blk - 1, 0), 0)),
                  pl.BlockSpec((None, 1, ns), lambda b, c: (b, 0, 0))] + [cs(a) for a in consts],
        out_specs=[row(dr)] * 6,
        out_shape=[out] * 6,
        compiler_params=_params("arbitrary", "arbitrary"),
        name="wkv_prep",
    )(ps, ps, shift_prev, *consts)


def _wkv_scan_body(rp_ref, y0_ref, pc_ref, hinc_ref, bonus_ref, g_ref, h0_ref, gnw_ref, gnb_ref, ones_ref,
                   y_ref, hout_ref, h_scr, y_scr, *, dr, passes):
    c = pl.program_id(1)

    @pl.when(c == 0)
    def _():
        h_scr[...] = h0_ref[...]

    m0 = lax.broadcasted_iota(jnp.int32, (CHUNK, PAIR), 1) < HEAD
    for ch in range(rp_ref.shape[0] // CHUNK):
        rs = slice(ch * CHUNK, (ch + 1) * CHUNK)
        for j in range(dr // PAIR):
            sl = slice(j * PAIR, (j + 1) * PAIR)
            lhs = jnp.concatenate([rp_ref[rs, sl], pc_ref[rs, sl]], axis=0)
            out = _mm(lhs, _bd2(h_scr[:, sl], m0), NN, passes)
            y_scr[rs, sl] = y0_ref[rs, sl] + out[:CHUNK]
            h_scr[:, sl] = hinc_ref[rs, sl] + out[CHUNK:]
    hout_ref[...] = h_scr[...]

    y = y_scr[...]
    ones = ones_ref[...]
    mean = _segsum(y, ones) * (1.0 / HEAD)
    d = y - mean
    var = _segsum(d * d, ones) * (1.0 / HEAD)
    yn = d * lax.rsqrt(var + GN_EPS) * gnw_ref[...] + gnb_ref[...]
    y_ref[...] = ((yn + bonus_ref[...]) * g_ref[...]).astype(BF16)


def _wkv_scan(rp, y0, pc, hinc, bonus, g, h0, gn_w, gn_b, ones_pair, batch, seq, passes):
    n, dr = rp.shape
    rows = min(SCAN_ROWS, seq)
    nc = seq // rows
    row = pl.BlockSpec((rows, dr), lambda b, c: (b * nc + c, 0))
    st = pl.BlockSpec((None, HEAD, dr), lambda b, c: (b, 0, 0))
    return pl.pallas_call(
        functools.partial(_wkv_scan_body, dr=dr, passes=passes),
        grid=(batch, nc),
        in_specs=[row] * 6 + [st, _const_spec((1, dr), 2), _const_spec((1, dr), 2),
                              _const_spec(ones_pair.shape, 2)],
        out_specs=[row, st],
        out_shape=[jax.ShapeDtypeStruct((n, dr), BF16), jax.ShapeDtypeStruct((batch, HEAD, dr), F32)],
        scratch_shapes=[pltpu.VMEM((HEAD, dr), F32), pltpu.VMEM((rows, dr), F32)],
        compiler_params=_params("arbitrary", "arbitrary"),
        name="wkv_scan",
    )(rp, y0, pc, hinc, bonus, g, h0, gn_w, gn_b, ones_pair)


def _flash_init(m_scr, acc_scr):
    m_scr[...] = jnp.full(m_scr.shape, NEG_BIG, F32)
    acc_scr[...] = jnp.zeros(acc_scr.shape, F32)


def _lane_rep(x, width):
    if width % PAIR == 0:
        return x if width == PAIR else jnp.concatenate([x] * (width // PAIR), axis=1)
    return x[:, :width]


def _flash_update(q, k, v, mask, m_scr, acc_scr):
    tq, tk = q.shape[0], k.shape[0]
    first = lax.broadcasted_iota(jnp.int32, (tq, PAIR), 1) < HEAD
    zq = jnp.zeros((tq, PAIR), BF16)
    ones = jnp.ones((tk, PAIR), BF16)
    n_maps = 2 * (q.shape[1] // PAIR)

    def scores(i):
        sl = slice((i // 2) * PAIR, (i // 2 + 1) * PAIR)
        qh = q[:, sl]
        qm = jnp.where(first, qh, zq) if i % 2 == 0 else jnp.where(first, zq, qh)
        return _dg(qm, k[:, sl], NT)

    s_next = scores(0)
    for i in range(n_maps):
        s = s_next
        if i + 1 < n_maps:
            s_next = scores(i + 1)
        if mask is not None:
            s = jnp.where(mask, s, NEG_BIG)
        sl = slice((i // 2) * PAIR, (i // 2 + 1) * PAIR)
        v_aug = jnp.concatenate([v[:, sl], ones], axis=1)
        m_prev = m_scr[i]
        m_new = jnp.maximum(m_prev, jnp.max(s, axis=1, keepdims=True))
        alpha = jnp.exp2(m_prev - m_new)
        p = jnp.exp2((s - _lane_rep(m_new, tk)).astype(BF16))
        acc_scr[i] = _lane_rep(alpha, 2 * PAIR) * acc_scr[i] + _dg(p, v_aug, NN)
        m_scr[i] = m_new


def _flash_finish(lam, sub_ref, scale, y_ref, acc_scr):
    for h in range(y_ref.shape[1] // PAIR):
        a1, a2 = acc_scr[2 * h], acc_scr[2 * h + 1]
        o = a1[:, :PAIR] / a1[:, PAIR:] - lam * (a2[:, :PAIR] / a2[:, PAIR:])
        ms = jnp.mean(o * o, axis=-1, keepdims=True)
        y = o * lax.rsqrt(ms + NORM_EPS) * sub_ref[...] * scale
        y_ref[:, h * PAIR:(h + 1) * PAIR] = y.astype(BF16)


def _attn_prompt_body(lam_ref, q_ref, k_ref, v_ref, sub_ref, y_ref, m_scr, acc_scr, *, tq, scale):
    qi = pl.program_id(1)
    kj = pl.program_id(2)

    @pl.when(kj == 0)
    def _():
        _flash_init(m_scr, acc_scr)

    @pl.when(kj < qi)
    def _():
        _flash_update(q_ref[...], k_ref[...], v_ref[...], None, m_scr, acc_scr)

    @pl.when(kj == qi)
    def _():
        qrow = lax.broadcasted_iota(jnp.int32, (tq, tq), 0)
        kcol = lax.broadcasted_iota(jnp.int32, (tq, tq), 1)
        mask = (kcol // CHUNK) <= (qrow // CHUNK)
        _flash_update(q_ref[...], k_ref[...], v_ref[...], mask, m_scr, acc_scr)

    @pl.when(kj == pl.num_programs(2) - 1)
    def _():
        _flash_finish(lam_ref[0], sub_ref, scale, y_ref, acc_scr)


def _attn_prompt(lam, q, kb, vb, subln, scale, batch, seq):
    n, dd = q.shape
    tq = min(512, seq)
    nq = seq // tq
    nh2 = 2 * dd // PAIR
    kv_spec = pl.BlockSpec((tq, dd), lambda b, i, j: (b * nq + jnp.minimum(j, i), 0))
    q_spec = pl.BlockSpec((tq, dd), lambda b, i, j: (b * nq + i, 0))
    return pl.pallas_call(
        functools.partial(_attn_prompt_body, tq=tq, scale=scale),
        grid=(batch, nq, nq),
        in_specs=[pl.BlockSpec(memory_space=pltpu.SMEM), q_spec, kv_spec, kv_spec, _const_spec((1, PAIR), 3)],
        out_specs=q_spec,
        out_shape=jax.ShapeDtypeStruct((n, dd), BF16),
        scratch_shapes=[pltpu.VMEM((nh2, tq, PAIR), F32), pltpu.VMEM((nh2, tq, 2 * PAIR), F32)],
        compiler_params=_params("arbitrary", "arbitrary", "arbitrary"),
        name="attn_prompt",
    )(lam, q, kb, vb, subln)


def _attn_sample_body(lam_ref, q_ref, ck_ref, cv_ref, k_ref, v_ref, sub_ref, y_ref, m_scr, acc_scr, *, scale):
    kj = pl.program_id(1)
    last = pl.num_programs(1) - 1

    @pl.when(kj == 0)
    def _():
        _flash_init(m_scr, acc_scr)

    @pl.when(kj < last)
    def _():
        _flash_update(q_ref[...], ck_ref[...].astype(BF16), cv_ref[...].astype(BF16), None, m_scr, acc_scr)

    @pl.when(kj == last)
    def _():
        _flash_update(q_ref[...], k_ref[...], v_ref[...], None, m_scr, acc_scr)
        _flash_finish(lam_ref[0], sub_ref, scale, y_ref, acc_scr)


def _attn_sample(lam, q, cache_k, cache_v, layer, kb, vb, subln, scale, batch, seq):
    n, dd = q.shape
    past = cache_k.shape[2]
    tk = min(1024, past)
    nk = past // tk
    nh2 = 2 * dd // PAIR
    row = pl.BlockSpec((seq, dd), lambda b, j: (b, 0))
    cache = pl.BlockSpec((None, None, tk, dd), lambda b, j: (layer, b, jnp.minimum(j, nk - 1), 0))
    return pl.pallas_call(
        functools.partial(_attn_sample_body, scale=scale),
        grid=(batch, nk + 1),
        in_specs=[pl.BlockSpec(memory_space=pltpu.SMEM), row, cache, cache, row, row, _const_spec((1, PAIR), 2)],
        out_specs=row,
        out_shape=jax.ShapeDtypeStruct((n, dd), BF16),
        scratch_shapes=[pltpu.VMEM((nh2, seq, PAIR), F32), pltpu.VMEM((nh2, seq, 2 * PAIR), F32)],
        compiler_params=_params("arbitrary", "arbitrary"),
        name="attn_sample",
    )(lam, q, cache_k, cache_v, kb, vb, subln)


def _ffn_body(h_ref, yr_ref, ya_ref, wo_ref, gf_ref, wup_ref, cw_ref, cb_ref, wdn_ref, cprev_ref,
              hout_ref, cnew_ref, carry_scr, *, dff, fc):
    t = pl.program_id(1)

    @pl.when(t == 0)
    def _():
        carry_scr[...] = cprev_ref[...]

    y = jnp.concatenate([yr_ref[...], ya_ref[...]], axis=1)
    h1 = h_ref[...] + _dg(y, wo_ref[...], NN)
    ms = jnp.mean(h1 * h1, axis=-1, keepdims=True)
    xn = (h1 * lax.rsqrt(ms + NORM_EPS) * gf_ref[...]).astype(BF16)
    tm = h1.shape[0]
    rowi = lax.broadcasted_iota(jnp.int32, (tm, fc), 0)
    acc = jnp.zeros(h1.shape, F32)

    def up(c):
        return (_dg(xn, wup_ref[:, c * fc:(c + 1) * fc], NN),
                _dg(xn, wup_ref[:, dff + c * fc:dff + (c + 1) * fc], NN))

    nxt = up(0)
    for c in range(dff // fc):
        sl = slice(c * fc, (c + 1) * fc)
        gt, u = nxt
        if c + 1 < dff // fc:
            nxt = up(c + 1)
        p1 = carry_scr[7:8, sl]
        p2 = carry_scr[6:7, sl]
        g1 = jnp.where(rowi == 0, p1, pltpu.roll(gt, 1, axis=0))
        g2 = jnp.where(rowi == 0, p2, jnp.where(rowi == 1, p1, pltpu.roll(gt, 2, axis=0)))
        gc = cb_ref[:, sl] + g2 * cw_ref[0:1, sl] + g1 * cw_ref[1:2, sl] + gt * cw_ref[2:3, sl]
        carry_scr[:, sl] = gt[tm - 8:, :]
        hh = (gc * jax.nn.sigmoid(gc) * u).astype(BF16)
        acc = acc + _dg(hh, wdn_ref[sl, :], NN)
    hout_ref[...] = h1 + acc
    cnew_ref[...] = carry_scr[...]


def _ffn(h, yr, ya, wo, g_ffn, wup, cw, cb, wdn, conv_prev8, batch, seq):
    n, d = h.shape
    dff = cb.shape[-1]
    fc = 256
    tm = min(512, seq)
    nt = seq // tm
    row = lambda w: pl.BlockSpec((tm, w), lambda b, t: (b * nt + t, 0))
    st = pl.BlockSpec((None, 8, dff), lambda b, t: (b, 0, 0))
    cs = lambda a: _const_spec(a.shape, 2)
    return pl.pallas_call(
        functools.partial(_ffn_body, dff=dff, fc=fc),
        grid=(batch, nt),
        in_specs=[row(d), row(yr.shape[1]), row(ya.shape[1]), cs(wo), cs(g_ffn), cs(wup), cs(cw), cs(cb), cs(wdn), st],
        out_specs=[row(d), st],
        out_shape=[jax.ShapeDtypeStruct((n, d), F32), jax.ShapeDtypeStruct((batch, 8, dff), F32)],
        scratch_shapes=[pltpu.VMEM((8, dff), F32)],
        compiler_params=_params("arbitrary", "arbitrary"),
        name="ffn",
    )(h, yr, ya, wo, g_ffn, wup, cw, cb, wdn, conv_prev8)


WKV_PASSES = 1
PREP_ROWS = 256
SCAN_ROWS = 256


def _lambda_init(layer):
    return 0.8 - 0.6 * math.exp(-0.3 * layer)


def _run_group(x, depth, layers, ones_pair, shift_prev, wkv_prev, conv_prev, attend):
    batch, seq, d = x.shape
    h = x.reshape(batch * seq, d)
    ks, vs, ws, ss, cs = [], [], [], [], []
    for l in range(depth):
        lp = layers[l]
        ns = lp["mu"].shape[-1]
        dd = lp["q_gain"].shape[-1]
        dr = lp["w0"].shape[-1]
        dff = lp["conv_b"].shape[-1]
        ps, q, k, v, kb, vb = _inproj(h, lp["g_mix"], lp["w_in"], lp["q_gain"], lp["k_gain"], ones_pair, ns, dd)

        sp = jnp.zeros((batch, 1, ns), F32) if shift_prev is None else shift_prev[l][:, None, :]
        if wkv_prev is None:
            h0 = jnp.zeros((batch, HEAD, dr), F32)
        else:
            h0 = jnp.transpose(wkv_prev[l], (0, 3, 1, 2)).reshape(batch, HEAD, dr)
        rp, y0, pc, hinc, bonus, g = _wkv_prep(ps, sp, lp, ones_pair, batch, seq, WKV_PASSES)
        yr, hout = _wkv_scan(rp, y0, pc, hinc, bonus, g, h0, lp["gn_w"], lp["gn_b"], ones_pair, batch, seq,
                             WKV_PASSES)

        lam_init = _lambda_init(l)
        ya = attend(l, lp["lam"], q, kb, vb, lp["subln"], 1.0 - lam_init, batch, seq)

        cp = jnp.zeros((batch, 8, dff), F32) if conv_prev is None else jnp.pad(conv_prev[l], ((0, 0), (6, 0), (0, 0)))
        h, cnew = _ffn(h, yr, ya, lp["w_out"], lp["g_ffn"], lp["w_up"], lp["conv_w"], lp["conv_b"], lp["w_down"],
                       cp, batch, seq)

        ks.append(k.reshape(batch, seq, dd // PAIR, PAIR))
        vs.append(v.reshape(batch, seq, dd // PAIR, PAIR))
        ws.append(jnp.transpose(hout.reshape(batch, HEAD, dr // HEAD, HEAD), (0, 2, 3, 1)))
        ss.append(ps.reshape(batch, seq, ns)[:, -1])
        cs.append(cnew[:, 6:8])
    return (h.reshape(batch, seq, d), jnp.stack(ks), jnp.stack(vs), jnp.stack(ws), jnp.stack(ss), jnp.stack(cs))


def kernel(x_prompt, x_sample, cache_k, cache_v, state_wkv, state_shift, state_conv, g_mix, w_in, mu_shift, w0, w_decay, a0, w_aaa, w_gate, k_k, k_a, r_k, gn_w, gn_b, q_gain, k_gain, lambdas, subln_gain, w_out, g_ffn, w_ffn_in, conv_w, conv_b, w_ffn_out):
    depth = w_in.shape[0]
    dr = w0.shape[-1]
    ns = mu_shift.shape[-1]
    dd = (w_in.shape[-1] - ns) // 3
    assert dr % PAIR == 0 and dd % PAIR == 0 and q_gain.shape[-1] == HEAD and subln_gain.shape[-1] == PAIR
    assert x_prompt.shape[1] % CHUNK == 0 and x_sample.shape[1] == CHUNK

    row = lambda a: a.reshape(1, -1).astype(F32)
    layers = []
    for l in range(depth):
        lv = lambdas[l].astype(F32)
        lam = jnp.exp(jnp.sum(lv[0] * lv[1])) - jnp.exp(jnp.sum(lv[2] * lv[3])) + _lambda_init(l)
        layers.append(dict(
            g_mix=row(g_mix[l]), w_in=w_in[l].astype(BF16), mu=row(mu_shift[l]), w0=row(w0[l]), a0=row(a0[l]),
            w_decay=w_decay[l], w_aaa=w_aaa[l], w_gate=w_gate[l], k_k=row(k_k[l]), k_a=row(k_a[l]), r_k=row(r_k[l]),
            gn_w=row(gn_w[l]), gn_b=row(gn_b[l]),
            q_gain=row(jnp.tile(q_gain[l], dd // HEAD)), k_gain=row(jnp.tile(k_gain[l], dd // HEAD)),
            lam=lam.reshape(1), subln=row(subln_gain[l]), w_out=w_out[l].astype(BF16), g_ffn=row(g_ffn[l]),
            w_up=w_ffn_in[l].astype(BF16), conv_w=conv_w[l], conv_b=row(conv_b[l]), w_down=w_ffn_out[l].astype(BF16)))

    ones_pair = jnp.kron(jnp.eye(2, dtype=F32), jnp.ones((HEAD, HEAD), F32)).astype(BF16)

    def attend_prompt(l, lam, q, kb, vb, subln, scale, batch, seq):
        return _attn_prompt(lam, q, kb, vb, subln, scale, batch, seq)

    ck = cache_k.reshape(cache_k.shape[:3] + (dd,))
    cv = cache_v.reshape(cache_v.shape[:3] + (dd,))

    def attend_sample(l, lam, q, kb, vb, subln, scale, batch, seq):
        return _attn_sample(lam, q, ck, cv, l, kb, vb, subln, scale, batch, seq)

    yp, pk, pv, pw, ps_, pc = _run_group(x_prompt, depth, layers, ones_pair, None, None, None, attend_prompt)
    ys, sk, sv, sw, ss, sc = _run_group(x_sample, depth, layers, ones_pair, state_shift, state_wkv, state_conv,
                                        attend_sample)
    return (yp, ys, pk, pv, pw, ps_, pc, sk, sv, sw, ss, sc)
```

```python
import functools
import math

import jax
import jax.numpy as jnp
from jax import lax
from jax.experimental import pallas as pl
from jax.experimental.pallas import tpu as pltpu

F32 = jnp.float32
BF16 = jnp.bfloat16

HEAD = 64
PAIR = 2 * HEAD
CHUNK = 64
NORM_EPS = 1e-6
GN_EPS = 64e-5
NEG_BIG = -1e30
LOG2E = math.log2(math.e)
VMEM_LIMIT = 56 * 1024 * 1024

NN = (((1,), (0,)), ((), ()))
NT = (((1,), (1,)), ((), ()))
TN = (((0,), (0,)), ((), ()))


def _dg(a, b, dn):
    return lax.dot_general(a, b, dn, preferred_element_type=F32)


def _split2(x):
    hi = x.astype(BF16)
    lo = (x - hi.astype(F32)).astype(BF16)
    return hi, lo


def _mm(a, b, dn=NN):
    return _dg(a.astype(BF16), b.astype(BF16), dn)


def _mm3(a, b):
    ah, al = _split2(a)
    bh, bl = _split2(b)
    return _dg(ah, bh, NN) + (_dg(ah, bl, NN) + _dg(al, bh, NN))


def _mm_exact_rhs(a, e):
    hi, lo = _split2(a)
    return _dg(hi, e, NN) + _dg(lo, e, NN)


def _mm_exact_lhs(e, b):
    hi, lo = _split2(b)
    return _dg(e, hi, NN) + _dg(e, lo, NN)


def _segsum(x, ones_bd):
    w = ones_bd.shape[0]
    parts = [_mm_exact_rhs(x[:, i:i + w], ones_bd) for i in range(0, x.shape[1], w)]
    return parts[0] if len(parts) == 1 else jnp.concatenate(parts, axis=1)


def _params(*sem):
    return pltpu.CompilerParams(dimension_semantics=sem, vmem_limit_bytes=VMEM_LIMIT)


def _const_spec(shape, grid_rank):
    zeros = (0,) * len(shape)
    if grid_rank == 1:
        return pl.BlockSpec(shape, lambda i: zeros, pipeline_mode=pl.Buffered(1))
    if grid_rank == 2:
        return pl.BlockSpec(shape, lambda i, j: zeros, pipeline_mode=pl.Buffered(1))
    return pl.BlockSpec(shape, lambda i, j, k: zeros, pipeline_mode=pl.Buffered(1))


def _inproj_body(x_ref, g_ref, w_ref, qg_ref, kg_ref, ones_ref,
                 ps_ref, q_ref, k_ref, v_ref, kb_ref, vb_ref, *, ns, dd):
    x = x_ref[...]
    ms = jnp.mean(x * x, axis=-1, keepdims=True)
    xn = (x * lax.rsqrt(ms + NORM_EPS) * g_ref[...]).astype(BF16)
    ps_ref[...] = _dg(xn, w_ref[:, :ns], NN)
    ones = ones_ref[...]

    def head_norm(t, gain):
        w = ones.shape[0]
        sq = (t * t).astype(BF16)
        ss = jnp.concatenate([_dg(sq[:, i:i + w], ones, NN) for i in range(0, dd, w)], axis=1)
        return t * lax.rsqrt(ss * (1.0 / HEAD) + NORM_EPS) * gain

    q = _dg(xn, w_ref[:, ns:ns + dd], NN)
    k = _dg(xn, w_ref[:, ns + dd:ns + 2 * dd], NN)
    v = _dg(xn, w_ref[:, ns + 2 * dd:ns + 3 * dd], NN)
    v_ref[...] = v
    vb_ref[...] = v.astype(BF16)
    q = head_norm(q, qg_ref[...])
    q_ref[...] = (q * (HEAD ** -0.5 * LOG2E)).astype(BF16)
    k = head_norm(k, kg_ref[...])
    k_ref[...] = k
    kb_ref[...] = k.astype(BF16)


def _inproj(h, g, w_bf, q_gain, k_gain, ns, dd):
    n, d = h.shape
    tm = min(512, n)
    ones = jnp.kron(jnp.eye(2 * PAIR // HEAD, dtype=F32), jnp.ones((HEAD, HEAD), F32)).astype(BF16)
    row = lambda w: pl.BlockSpec((tm, w), lambda i: (i, 0))
    return pl.pallas_call(
        functools.partial(_inproj_body, ns=ns, dd=dd),
        grid=(n // tm,),
        in_specs=[row(d), _const_spec((1, d), 1), _const_spec(w_bf.shape, 1),
                  _const_spec((1, dd), 1), _const_spec((1, dd), 1), _const_spec(ones.shape, 1)],
        out_specs=[row(ns), row(dd), row(dd), row(dd), row(dd), row(dd)],
        out_shape=[jax.ShapeDtypeStruct((n, ns), F32), jax.ShapeDtypeStruct((n, dd), BF16),
                   jax.ShapeDtypeStruct((n, dd), F32), jax.ShapeDtypeStruct((n, dd), F32),
                   jax.ShapeDtypeStruct((n, dd), BF16), jax.ShapeDtypeStruct((n, dd), BF16)],
        compiler_params=_params("arbitrary"),
        name="inproj",
    )(h, g, w_bf, q_gain, k_gain, ones)


def _bd2(x, m0, swap=False):
    zero = jnp.zeros_like(x)
    first = jnp.where(m0, x, zero)
    second = jnp.where(m0, zero, x)
    return jnp.concatenate([second, first] if swap else [first, second], axis=0)


def _wkv_prep_body(ps_ref, prev_ref, sp_ref, mu_ref, w0_ref, a0_ref, kk_ref, ka_ref, rk_ref,
                   wd_ref, wa_ref, wg_ref, ones_ref, tril_ref,
                   rp_ref, y0_ref, pc_ref, hinc_ref, bonus_ref, g_ref, *, dr, nd, na):
    c = pl.program_id(1)
    x = ps_ref[...]
    row0 = jnp.where(c == 0, sp_ref[...], prev_ref[7:8, :])
    rowi = lax.broadcasted_iota(jnp.int32, x.shape, 0)
    prev = jnp.where(rowi == 0, row0, pltpu.roll(x, 1, axis=0))
    xs = x + (prev - x) * mu_ref[...]
    r = xs[:, :dr]
    k = xs[:, dr:2 * dr]
    v = xs[:, 2 * dr:3 * dr]
    o = 3 * dr
    xw = xs[:, o:o + nd]
    xa = xs[:, o + nd:o + nd + na]
    xg = xs[:, o + nd + na:]

    z = w0_ref[...] + _mm3(jnp.tanh(xw), wd_ref[...])
    softplus = jnp.maximum(-z, 0.0) + jnp.log1p(jnp.exp(-jnp.abs(z)))
    ld = -jnp.exp(-softplus - 0.5)
    a = jax.nn.sigmoid(a0_ref[...] + _mm3(xa, wa_ref[...]))
    g_ref[...] = _mm(jax.nn.sigmoid(xg), wg_ref[...]).astype(BF16)

    ones = ones_ref[...]
    kk = k * kk_ref[...]
    kk = kk / jnp.maximum(jnp.sqrt(_segsum(kk * kk, ones)), 1e-12)
    k2 = k * (1.0 + (a - 1.0) * ka_ref[...])
    bonus_ref[...] = (_segsum(r * k2 * rk_ref[...], ones) * v).astype(BF16)
    av = -kk
    bv = kk * a

    cum = _mm_exact_lhs(tril_ref[...], ld)
    e_neg = jnp.exp(-cum)
    at_all = av * jnp.exp(cum - ld)
    rt_all = r * jnp.exp(cum)
    bt_all = bv * e_neg
    kt_all = k2 * e_neg

    lane = lax.broadcasted_iota(jnp.int32, (CHUNK, PAIR), 1)
    rr = lax.broadcasted_iota(jnp.int32, (CHUNK, PAIR), 0)
    m0 = lane < HEAD
    lane_in = jnp.where(m0, lane, lane - HEAD)
    strict = rr > lane_in
    incl = rr >= lane_in
    diag = rr == lane_in
    eye2 = jnp.where(diag, 1.0, 0.0).astype(F32)
    m0w = lax.broadcasted_iota(jnp.int32, (2 * CHUNK, PAIR), 1) < HEAD
    zero = jnp.zeros((CHUNK, PAIR), F32)
    zero2 = jnp.zeros((2 * CHUNK, PAIR), F32)

    bd = lambda t, swap=False: _bd2(t, m0, swap).astype(BF16)

    chains = [(slice(ch * CHUNK, (ch + 1) * CHUNK), slice(j * PAIR, (j + 1) * PAIR), (ch + 1) * CHUNK - 1)
              for ch in range(x.shape[0] // CHUNK) for j in range(dr // PAIR)]
    atp = [at_all[rs, sl] for rs, sl, _ in chains]
    rtp = [rt_all[rs, sl] for rs, sl, _ in chains]
    vp = [v[rs, sl] for rs, sl, _ in chains]
    g0, g1 = [], []
    for i, (rs, sl, _) in enumerate(chains):
        btp, ktp = bt_all[rs, sl], kt_all[rs, sl]
        lhs = jnp.concatenate([atp[i], rtp[i]], axis=0)
        g0.append(_mm(jnp.where(m0w, lhs, zero2), jnp.concatenate([btp, ktp], axis=0), NT))
        g1.append(_mm(jnp.where(m0w, zero2, lhs), jnp.concatenate([ktp, btp], axis=0), NT))
    qp = [jnp.where(strict, jnp.where(m0, a[:CHUNK], b[:CHUNK]), zero) for a, b in zip(g0, g1)]
    akmk = [jnp.concatenate([jnp.where(strict, jnp.where(m0, b[:CHUNK], a[:CHUNK]), zero),
                             jnp.where(incl, jnp.where(m0, b[CHUNK:], a[CHUNK:]), zero)], axis=0)
            for a, b in zip(g0, g1)]
    mb = [jnp.where(incl, jnp.where(m0, a[CHUNK:], b[CHUNK:]), zero) for a, b in zip(g0, g1)]
    kv = [_mm(l, bd(t, True)) for l, t in zip(akmk, vp)]

    tm = [eye2 + t for t in qp]
    bq = [bd(t) for t in qp]
    for _ in range(5):
        qp = [_mm(t, b) for t, b in zip(qp, bq)]
        bq = [bd(t) for t in qp]
        tm = [t + _mm(t, b) for t, b in zip(tm, bq)]

    au = [_mm(t, jnp.concatenate([bd(a), bd(k[:CHUNK])], axis=1)) for t, a, k in zip(tm, atp, kv)]
    ry = [_mm(t, jnp.concatenate([bd(a[:, :PAIR]), bd(a[:, PAIR:])], axis=1)) for t, a in zip(mb, au)]
    ph = []
    for i, (rs, sl, last) in enumerate(chains):
        e_end = jnp.exp(cum[last:last + 1, sl] - cum[rs, sl])
        rhs = jnp.concatenate([au[i], jnp.concatenate([zero, vp[i]], axis=1)], axis=0)
        ph.append(_mm(jnp.concatenate([bv[rs, sl] * e_end, k2[rs, sl] * e_end], axis=0), rhs, TN))
    for i, (rs, sl, last) in enumerate(chains):
        rp_ref[rs, sl] = (rtp[i] + ry[i][:, :PAIR]).astype(BF16)
        y0_ref[rs, sl] = kv[i][CHUNK:] + ry[i][:, PAIR:]
        g_c = jnp.exp(cum[last:last + 1, sl])
        pc_ref[rs, sl] = (jnp.where(m0, ph[i][:CHUNK, :PAIR], ph[i][CHUNK:, :PAIR])
                          + jnp.where(diag, g_c, 0.0)).astype(BF16)
        hinc_ref[rs, sl] = jnp.where(m0, ph[i][:CHUNK, PAIR:], ph[i][CHUNK:, PAIR:])


def _wkv_prep(ps, shift_prev, lp, ones_pair, batch, seq):
    n, ns = ps.shape
    dr = lp["w0"].shape[-1]
    nd = lp["w_decay"].shape[0]
    na = lp["w_aaa"].shape[0]
    rows = min(PREP_ROWS, seq)
    nc = seq // rows
    blk = rows // 8
    tril = jnp.kron(jnp.eye(rows // CHUNK, dtype=F32), jnp.tril(jnp.ones((CHUNK, CHUNK), F32))).astype(BF16)
    row = lambda w: pl.BlockSpec((rows, w), lambda b, c: (b * nc + c, 0))
    cs = lambda a: _const_spec(a.shape, 2)
    out = lambda dt: jax.ShapeDtypeStruct((n, dr), dt)
    consts = [lp["mu"], lp["w0"], lp["a0"], lp["k_k"], lp["k_a"], lp["r_k"],
              lp["w_decay"], lp["w_aaa"], lp["w_gate"], ones_pair, tril]
    return pl.pallas_call(
        functools.partial(_wkv_prep_body, dr=dr, nd=nd, na=na),
        grid=(batch, nc),
        in_specs=[row(ns),
                  pl.BlockSpec((8, ns), lambda b, c: (jnp.maximum((b * nc + c) * blk - 1, 0), 0)),
                  pl.BlockSpec((None, 1, ns), lambda b, c: (b, 0, 0))] + [cs(a) for a in consts],
        out_specs=[row(dr)] * 6,
        out_shape=[out(BF16), out(F32), out(BF16), out(F32), out(BF16), out(BF16)],
        compiler_params=_params("arbitrary", "arbitrary"),
        name="wkv_prep",
    )(ps, ps, shift_prev, *consts)


def _wkv_scan_body(rp_ref, y0_ref, pc_ref, hinc_ref, bonus_ref, g_ref, h0_ref, gnw_ref, gnb_ref, ones_ref,
                   y_ref, hout_ref, h_scr, y_scr, *, dr):
    c = pl.program_id(1)

    @pl.when(c == 0)
    def _():
        h_scr[...] = h0_ref[...]

    m0 = lax.broadcasted_iota(jnp.int32, (CHUNK, PAIR), 1) < HEAD
    for ch in range(rp_ref.shape[0] // CHUNK):
        rs = slice(ch * CHUNK, (ch + 1) * CHUNK)
        for j in range(dr // PAIR):
            sl = slice(j * PAIR, (j + 1) * PAIR)
            lhs = jnp.concatenate([rp_ref[rs, sl], pc_ref[rs, sl]], axis=0)
            out = _mm(lhs, _bd2(h_scr[:, sl], m0))
            y_scr[rs, sl] = y0_ref[rs, sl] + out[:CHUNK]
            h_scr[:, sl] = hinc_ref[rs, sl] + out[CHUNK:]
    hout_ref[...] = h_scr[...]

    y = y_scr[...]
    ones = ones_ref[...]
    mean = _segsum(y, ones) * (1.0 / HEAD)
    d = y - mean
    var = _segsum(d * d, ones) * (1.0 / HEAD)
    yn = d * lax.rsqrt(var + GN_EPS) * gnw_ref[...] + gnb_ref[...]
    y_ref[...] = ((yn + bonus_ref[...]) * g_ref[...]).astype(BF16)


def _wkv_scan(rp, y0, pc, hinc, bonus, g, h0, gn_w, gn_b, ones_pair, batch, seq):
    n, dr = rp.shape
    rows = min(SCAN_ROWS, seq)
    nc = seq // rows
    row = pl.BlockSpec((rows, dr), lambda b, c: (b * nc + c, 0))
    st = pl.BlockSpec((None, HEAD, dr), lambda b, c: (b, 0, 0))
    return pl.pallas_call(
        functools.partial(_wkv_scan_body, dr=dr),
        grid=(batch, nc),
        in_specs=[row] * 6 + [st, _const_spec((1, dr), 2), _const_spec((1, dr), 2),
                              _const_spec(ones_pair.shape, 2)],
        out_specs=[row, st],
        out_shape=[jax.ShapeDtypeStruct((n, dr), BF16), jax.ShapeDtypeStruct((batch, HEAD, dr), F32)],
        scratch_shapes=[pltpu.VMEM((HEAD, dr), F32), pltpu.VMEM((rows, dr), F32)],
        compiler_params=_params("arbitrary", "arbitrary"),
        name="wkv_scan",
    )(rp, y0, pc, hinc, bonus, g, h0, gn_w, gn_b, ones_pair)


def _flash_init(m_scr, acc_scr):
    m_scr[...] = jnp.full(m_scr.shape, NEG_BIG, F32)
    acc_scr[...] = jnp.zeros(acc_scr.shape, F32)


def _lane_rep(x, width):
    if width % PAIR == 0:
        return x if width == PAIR else jnp.concatenate([x] * (width // PAIR), axis=1)
    return x[:, :width]


def _flash_update(q, k, v, mask, m_scr, acc_scr):
    tq, tk = q.shape[0], k.shape[0]
    first = lax.broadcasted_iota(jnp.int32, (tq, PAIR), 1) < HEAD
    zq = jnp.zeros((tq, PAIR), BF16)
    ones = jnp.ones((tk, PAIR), BF16)
    n_maps = 2 * (q.shape[1] // PAIR)

    def scores(i):
        sl = slice((i // 2) * PAIR, (i // 2 + 1) * PAIR)
        qh = q[:, sl]
        qm = jnp.where(first, qh, zq) if i % 2 == 0 else jnp.where(first, zq, qh)
        return _dg(qm, k[:, sl], NT)

    s_next = scores(0)
    for i in range(n_maps):
        s = s_next
        if i + 1 < n_maps:
            s_next = scores(i + 1)
        if mask is not None:
            s = jnp.where(mask, s, NEG_BIG)
        sl = slice((i // 2) * PAIR, (i // 2 + 1) * PAIR)
        v_aug = jnp.concatenate([v[:, sl], ones], axis=1)
        m_prev = m_scr[i]
        m_new = jnp.maximum(m_prev, jnp.max(s, axis=1, keepdims=True))
        alpha = jnp.exp2(m_prev - m_new)
        p = jnp.exp2((s - _lane_rep(m_new, tk)).astype(BF16))
        acc_scr[i] = _lane_rep(alpha, 2 * PAIR) * acc_scr[i] + _dg(p, v_aug, NN)
        m_scr[i] = m_new


def _flash_finish(lam, sub_ref, scale, y_ref, acc_scr):
    for h in range(y_ref.shape[1] // PAIR):
        a1, a2 = acc_scr[2 * h], acc_scr[2 * h + 1]
        o = a1[:, :PAIR] / a1[:, PAIR:] - lam * (a2[:, :PAIR] / a2[:, PAIR:])
        ms = jnp.mean(o * o, axis=-1, keepdims=True)
        y = o * lax.rsqrt(ms + NORM_EPS) * sub_ref[...] * scale
        y_ref[:, h * PAIR:(h + 1) * PAIR] = y.astype(BF16)


def _attn_prompt_body(qt_ref, kt_ref, lam_ref, q_ref, k_ref, v_ref, sub_ref, y_ref, m_scr, acc_scr, *, tq, scale):
    s = pl.program_id(1)
    qi = qt_ref[s]
    kj = kt_ref[s]

    @pl.when(kj == 0)
    def _():
        _flash_init(m_scr, acc_scr)

    @pl.when(kj < qi)
    def _():
        _flash_update(q_ref[...], k_ref[...], v_ref[...], None, m_scr, acc_scr)

    @pl.when(kj == qi)
    def _():
        qrow = lax.broadcasted_iota(jnp.int32, (tq, tq), 0)
        kcol = lax.broadcasted_iota(jnp.int32, (tq, tq), 1)
        mask = (kcol // CHUNK) <= (qrow // CHUNK)
        _flash_update(q_ref[...], k_ref[...], v_ref[...], mask, m_scr, acc_scr)
        _flash_finish(lam_ref[0], sub_ref, scale, y_ref, acc_scr)


def _attn_prompt(lam, q, kb, vb, subln, scale, batch, seq):
    n, dd = q.shape
    tq = min(512, seq)
    nq = seq // tq
    nh2 = 2 * dd // PAIR
    pairs = [(i, j) for i in range(nq) for j in range(i + 1)]
    qt = jnp.array([p[0] for p in pairs], jnp.int32)
    kt = jnp.array([p[1] for p in pairs], jnp.int32)
    kv_spec = pl.BlockSpec((tq, dd), lambda b, s, qt, kt: (b * nq + kt[s], 0))
    q_spec = pl.BlockSpec((tq, dd), lambda b, s, qt, kt: (b * nq + qt[s], 0))
    return pl.pallas_call(
        functools.partial(_attn_prompt_body, tq=tq, scale=scale),
        grid_spec=pltpu.PrefetchScalarGridSpec(
            num_scalar_prefetch=2,
            grid=(batch, len(pairs)),
            in_specs=[pl.BlockSpec(memory_space=pltpu.SMEM), q_spec, kv_spec, kv_spec,
                      pl.BlockSpec((1, PAIR), lambda b, s, qt, kt: (0, 0))],
            out_specs=q_spec,
            scratch_shapes=[pltpu.VMEM((nh2, tq, PAIR), F32), pltpu.VMEM((nh2, tq, 2 * PAIR), F32)]),
        out_shape=jax.ShapeDtypeStruct((n, dd), BF16),
        compiler_params=_params("arbitrary", "arbitrary"),
        name="attn_prompt",
    )(qt, kt, lam, q, kb, vb, subln)


def _attn_sample_body(lam_ref, q_ref, ck_ref, cv_ref, k_ref, v_ref, sub_ref, y_ref, m_scr, acc_scr, *, scale):
    kj = pl.program_id(1)
    last = pl.num_programs(1) - 1

    @pl.when(kj == 0)
    def _():
        _flash_init(m_scr, acc_scr)

    @pl.when(kj < last)
    def _():
        _flash_update(q_ref[...], ck_ref[...].astype(BF16), cv_ref[...].astype(BF16), None, m_scr, acc_scr)

    @pl.when(kj == last)
    def _():
        _flash_update(q_ref[...], k_ref[...], v_ref[...], None, m_scr, acc_scr)
        _flash_finish(lam_ref[0], sub_ref, scale, y_ref, acc_scr)


def _attn_sample(lam, q, cache_k, cache_v, layer, kb, vb, subln, scale, batch, seq):
    n, dd = q.shape
    past = cache_k.shape[2]
    tk = min(1024, past)
    nk = past // tk
    nh2 = 2 * dd // PAIR
    row = pl.BlockSpec((seq, dd), lambda b, j: (b, 0))
    cache = pl.BlockSpec((None, None, tk, dd), lambda b, j: (layer, b, jnp.minimum(j, nk - 1), 0))
    return pl.pallas_call(
        functools.partial(_attn_sample_body, scale=scale),
        grid=(batch, nk + 1),
        in_specs=[pl.BlockSpec(memory_space=pltpu.SMEM), row, cache, cache, row, row, _const_spec((1, PAIR), 2)],
        out_specs=row,
        out_shape=jax.ShapeDtypeStruct((n, dd), BF16),
        scratch_shapes=[pltpu.VMEM((nh2, seq, PAIR), F32), pltpu.VMEM((nh2, seq, 2 * PAIR), F32)],
        compiler_params=_params("arbitrary", "arbitrary"),
        name="attn_sample",
    )(lam, q, cache_k, cache_v, kb, vb, subln)


def _ffn_body(h_ref, yr_ref, ya_ref, wo_ref, gf_ref, wup_ref, cw_ref, cb_ref, wdn_ref, cprev_ref,
              hout_ref, cnew_ref, carry_scr, *, dff, fc):
    t = pl.program_id(1)

    @pl.when(t == 0)
    def _():
        carry_scr[...] = cprev_ref[...]

    y = jnp.concatenate([yr_ref[...], ya_ref[...]], axis=1)
    h1 = h_ref[...] + _dg(y, wo_ref[...], NN)
    ms = jnp.mean(h1 * h1, axis=-1, keepdims=True)
    xn = (h1 * lax.rsqrt(ms + NORM_EPS) * gf_ref[...]).astype(BF16)
    tm = h1.shape[0]
    rowi = lax.broadcasted_iota(jnp.int32, (tm, fc), 0)
    acc = jnp.zeros(h1.shape, F32)

    def up(c):
        return (_dg(xn, wup_ref[:, c * fc:(c + 1) * fc], NN),
                _dg(xn, wup_ref[:, dff + c * fc:dff + (c + 1) * fc], NN))

    nxt = up(0)
    for c in range(dff // fc):
        sl = slice(c * fc, (c + 1) * fc)
        gt, u = nxt
        if c + 1 < dff // fc:
            nxt = up(c + 1)
        p1 = carry_scr[7:8, sl]
        p2 = carry_scr[6:7, sl]
        g1 = jnp.where(rowi == 0, p1, pltpu.roll(gt, 1, axis=0))
        g2 = jnp.where(rowi == 0, p2, jnp.where(rowi == 1, p1, pltpu.roll(gt, 2, axis=0)))
        gc = cb_ref[:, sl] + g2 * cw_ref[0:1, sl] + g1 * cw_ref[1:2, sl] + gt * cw_ref[2:3, sl]
        carry_scr[:, sl] = gt[tm - 8:, :]
        hh = (gc * jax.nn.sigmoid(gc) * u).astype(BF16)
        acc = acc + _dg(hh, wdn_ref[sl, :], NN)
    hout_ref[...] = h1 + acc
    cnew_ref[...] = carry_scr[...]


def _ffn(h, yr, ya, wo, g_ffn, wup, cw, cb, wdn, conv_prev8, batch, seq):
    n, d = h.shape
    dff = cb.shape[-1]
    fc = 256
    tm = min(512, seq)
    nt = seq // tm
    row = lambda w: pl.BlockSpec((tm, w), lambda b, t: (b * nt + t, 0))
    st = pl.BlockSpec((None, 8, dff), lambda b, t: (b, 0, 0))
    cs = lambda a: _const_spec(a.shape, 2)
    return pl.pallas_call(
        functools.partial(_ffn_body, dff=dff, fc=fc),
        grid=(batch, nt),
        in_specs=[row(d), row(yr.shape[1]), row(ya.shape[1]), cs(wo), cs(g_ffn), cs(wup), cs(cw), cs(cb), cs(wdn), st],
        out_specs=[row(d), st],
        out_shape=[jax.ShapeDtypeStruct((n, d), F32), jax.ShapeDtypeStruct((batch, 8, dff), F32)],
        scratch_shapes=[pltpu.VMEM((8, dff), F32)],
        compiler_params=_params("arbitrary", "arbitrary"),
        name="ffn",
    )(h, yr, ya, wo, g_ffn, wup, cw, cb, wdn, conv_prev8)


def _stack_heads_body(*refs, depth):
    layer = pl.program_id(0)
    o_ref = refs[depth]
    for i in range(depth):
        @pl.when(layer == i)
        def _(i=i):
            for h in range(o_ref.shape[1]):
                o_ref[:, h, :] = refs[i][:, h * PAIR:(h + 1) * PAIR]


def _stack_heads(xs):
    depth = len(xs)
    n, dd = xs[0].shape
    nh = dd // PAIR
    tm = min(1024, n)

    def in_spec(i):
        return pl.BlockSpec((tm, dd), lambda l, t: (jnp.where(l == i, t, 0), 0))

    return pl.pallas_call(
        functools.partial(_stack_heads_body, depth=depth),
        grid=(depth, n // tm),
        in_specs=[in_spec(i) for i in range(depth)],
        out_specs=pl.BlockSpec((None, tm, nh, PAIR), lambda l, t: (l, t, 0, 0)),
        out_shape=jax.ShapeDtypeStruct((depth, n, nh, PAIR), xs[0].dtype),
        compiler_params=_params("arbitrary", "arbitrary"),
        name="stack_heads",
    )(*xs)


PREP_ROWS = 256
SCAN_ROWS = 256


def _lambda_init(layer):
    return 0.8 - 0.6 * math.exp(-0.3 * layer)


def _run_group(x, depth, layers, ones_pair, shift_prev, wkv_prev, conv_prev, attend):
    batch, seq, d = x.shape
    h = x.reshape(batch * seq, d)
    ks, vs, ws, ss, cs = [], [], [], [], []
    for l in range(depth):
        lp = layers[l]
        ns = lp["mu"].shape[-1]
        dd = lp["q_gain"].shape[-1]
        dr = lp["w0"].shape[-1]
        dff = lp["conv_b"].shape[-1]
        ps, q, k, v, kb, vb = _inproj(h, lp["g_mix"], lp["w_in"], lp["q_gain"], lp["k_gain"], ns, dd)

        sp = jnp.zeros((batch, 1, ns), F32) if shift_prev is None else shift_prev[l][:, None, :]
        if wkv_prev is None:
            h0 = jnp.zeros((batch, HEAD, dr), F32)
        else:
            h0 = jnp.transpose(wkv_prev[l], (0, 3, 1, 2)).reshape(batch, HEAD, dr)
        rp, y0, pc, hinc, bonus, g = _wkv_prep(ps, sp, lp, ones_pair, batch, seq)
        yr, hout = _wkv_scan(rp, y0, pc, hinc, bonus, g, h0, lp["gn_w"], lp["gn_b"], ones_pair, batch, seq)

        lam_init = _lambda_init(l)
        ya = attend(l, lp["lam"], q, kb, vb, lp["subln"], 1.0 - lam_init, batch, seq)

        cp = jnp.zeros((batch, 8, dff), F32) if conv_prev is None else jnp.pad(conv_prev[l], ((0, 0), (6, 0), (0, 0)))
        h, cnew = _ffn(h, yr, ya, lp["w_out"], lp["g_ffn"], lp["w_up"], lp["conv_w"], lp["conv_b"], lp["w_down"],
                       cp, batch, seq)

        ks.append(k)
        vs.append(v)
        ws.append(jnp.transpose(hout.reshape(batch, HEAD, dr // HEAD, HEAD), (0, 2, 3, 1)))
        ss.append(ps.reshape(batch, seq, ns)[:, -1])
        cs.append(cnew[:, 6:8])
    heads = lambda xs: _stack_heads(xs).reshape(depth, batch, seq, -1, PAIR)
    return (h.reshape(batch, seq, d), heads(ks), heads(vs), jnp.stack(ws), jnp.stack(ss), jnp.stack(cs))


def kernel(x_prompt, x_sample, cache_k, cache_v, state_wkv, state_shift, state_conv, g_mix, w_in, mu_shift, w0, w_decay, a0, w_aaa, w_gate, k_k, k_a, r_k, gn_w, gn_b, q_gain, k_gain, lambdas, subln_gain, w_out, g_ffn, w_ffn_in, conv_w, conv_b, w_ffn_out):
    depth = w_in.shape[0]
    dr = w0.shape[-1]
    ns = mu_shift.shape[-1]
    dd = (w_in.shape[-1] - ns) // 3
    assert dr % PAIR == 0 and dd % PAIR == 0 and q_gain.shape[-1] == HEAD and subln_gain.shape[-1] == PAIR
    assert x_prompt.shape[1] % CHUNK == 0 and x_sample.shape[1] == CHUNK

    row = lambda a: a.reshape(1, -1).astype(F32)
    layers = []
    for l in range(depth):
        lv = lambdas[l].astype(F32)
        lam = jnp.exp(jnp.sum(lv[0] * lv[1])) - jnp.exp(jnp.sum(lv[2] * lv[3])) + _lambda_init(l)
        layers.append(dict(
            g_mix=row(g_mix[l]), w_in=w_in[l].astype(BF16), mu=row(mu_shift[l]), w0=row(w0[l]), a0=row(a0[l]),
            w_decay=w_decay[l], w_aaa=w_aaa[l], w_gate=w_gate[l], k_k=row(k_k[l]), k_a=row(k_a[l]), r_k=row(r_k[l]),
            gn_w=row(gn_w[l]), gn_b=row(gn_b[l]),
            q_gain=row(jnp.tile(q_gain[l], dd // HEAD)), k_gain=row(jnp.tile(k_gain[l], dd // HEAD)),
            lam=lam.reshape(1), subln=row(subln_gain[l]), w_out=w_out[l].astype(BF16), g_ffn=row(g_ffn[l]),
            w_up=w_ffn_in[l].astype(BF16), conv_w=conv_w[l], conv_b=row(conv_b[l]), w_down=w_ffn_out[l].astype(BF16)))

    ones_pair = jnp.kron(jnp.eye(2, dtype=F32), jnp.ones((HEAD, HEAD), F32)).astype(BF16)

    def attend_prompt(l, lam, q, kb, vb, subln, scale, batch, seq):
        return _attn_prompt(lam, q, kb, vb, subln, scale, batch, seq)

    ck = cache_k.reshape(cache_k.shape[:3] + (dd,))
    cv = cache_v.reshape(cache_v.shape[:3] + (dd,))

    def attend_sample(l, lam, q, kb, vb, subln, scale, batch, seq):
        return _attn_sample(lam, q, ck, cv, l, kb, vb, subln, scale, batch, seq)

    yp, pk, pv, pw, ps_, pc = _run_group(x_prompt, depth, layers, ones_pair, None, None, None, attend_prompt)
    ys, sk, sv, sw, ss, sc = _run_group(x_sample, depth, layers, ones_pair, state_shift, state_wkv, state_conv,
                                        attend_sample)
    return (yp, ys, pk, pv, pw, ps_, pc, sk, sv, sw, ss, sc)
```

```python
import functools
import math

import jax
import jax.numpy as jnp
from jax import lax
from jax.experimental import pallas as pl
from jax.experimental.pallas import tpu as pltpu

F32 = jnp.float32
BF16 = jnp.bfloat16

HEAD = 64
PAIR = 2 * HEAD
CHUNK = 64
NORM_EPS = 1e-6
GN_EPS = 64e-5
NEG_BIG = -1e30
LOG2E = math.log2(math.e)
VMEM_LIMIT = 56 * 1024 * 1024

NN = (((1,), (0,)), ((), ()))
NT = (((1,), (1,)), ((), ()))
TN = (((0,), (0,)), ((), ()))


def _dg(a, b, dn):
    return lax.dot_general(a, b, dn, preferred_element_type=F32)


def _split2(x):
    hi = x.astype(BF16)
    lo = (x - hi.astype(F32)).astype(BF16)
    return hi, lo


def _mm(a, b, dn=NN):
    return _dg(a.astype(BF16), b.astype(BF16), dn)


def _mm3(a, b):
    ah, al = _split2(a)
    bh, bl = _split2(b)
    return _dg(ah, bh, NN) + (_dg(ah, bl, NN) + _dg(al, bh, NN))


def _mm_exact_rhs(a, e):
    hi, lo = _split2(a)
    return _dg(hi, e, NN) + _dg(lo, e, NN)


def _mm_exact_lhs(e, b):
    hi, lo = _split2(b)
    return _dg(e, hi, NN) + _dg(e, lo, NN)


def _segsum(x, ones_bd):
    w = ones_bd.shape[0]
    parts = [_mm_exact_rhs(x[:, i:i + w], ones_bd) for i in range(0, x.shape[1], w)]
    return parts[0] if len(parts) == 1 else jnp.concatenate(parts, axis=1)


def _params(*sem):
    return pltpu.CompilerParams(dimension_semantics=sem, vmem_limit_bytes=VMEM_LIMIT)


def _const_spec(shape, grid_rank):
    zeros = (0,) * len(shape)
    if grid_rank == 1:
        return pl.BlockSpec(shape, lambda i: zeros, pipeline_mode=pl.Buffered(1))
    if grid_rank == 2:
        return pl.BlockSpec(shape, lambda i, j: zeros, pipeline_mode=pl.Buffered(1))
    return pl.BlockSpec(shape, lambda i, j, k: zeros, pipeline_mode=pl.Buffered(1))


def _inproj_body(x_ref, g_ref, w_ref, qg_ref, kg_ref, ones_ref,
                 ps_ref, q_ref, k_ref, v_ref, kb_ref, vb_ref, *, ns, dd):
    x = x_ref[...]
    ms = jnp.mean(x * x, axis=-1, keepdims=True)
    xn = (x * lax.rsqrt(ms + NORM_EPS) * g_ref[...]).astype(BF16)
    ps_ref[...] = _dg(xn, w_ref[:, :ns], NN)
    ones = ones_ref[...]

    def head_norm(t, gain):
        w = ones.shape[0]
        sq = (t * t).astype(BF16)
        ss = jnp.concatenate([_dg(sq[:, i:i + w], ones, NN) for i in range(0, dd, w)], axis=1)
        return t * lax.rsqrt(ss * (1.0 / HEAD) + NORM_EPS) * gain

    q = _dg(xn, w_ref[:, ns:ns + dd], NN)
    k = _dg(xn, w_ref[:, ns + dd:ns + 2 * dd], NN)
    v = _dg(xn, w_ref[:, ns + 2 * dd:ns + 3 * dd], NN)
    v_ref[...] = v
    vb_ref[...] = v.astype(BF16)
    q = head_norm(q, qg_ref[...])
    q_ref[...] = (q * (HEAD ** -0.5 * LOG2E)).astype(BF16)
    k = head_norm(k, kg_ref[...])
    k_ref[...] = k
    kb_ref[...] = k.astype(BF16)


def _inproj(h, g, w_bf, q_gain, k_gain, ns, dd):
    n, d = h.shape
    tm = min(512, n)
    ones = jnp.kron(jnp.eye(2 * PAIR // HEAD, dtype=F32), jnp.ones((HEAD, HEAD), F32)).astype(BF16)
    row = lambda w: pl.BlockSpec((tm, w), lambda i: (i, 0))
    return pl.pallas_call(
        functools.partial(_inproj_body, ns=ns, dd=dd),
        grid=(n // tm,),
        in_specs=[row(d), _const_spec((1, d), 1), _const_spec(w_bf.shape, 1),
                  _const_spec((1, dd), 1), _const_spec((1, dd), 1), _const_spec(ones.shape, 1)],
        out_specs=[row(ns), row(dd), row(dd), row(dd), row(dd), row(dd)],
        out_shape=[jax.ShapeDtypeStruct((n, ns), F32), jax.ShapeDtypeStruct((n, dd), BF16),
                   jax.ShapeDtypeStruct((n, dd), F32), jax.ShapeDtypeStruct((n, dd), F32),
                   jax.ShapeDtypeStruct((n, dd), BF16), jax.ShapeDtypeStruct((n, dd), BF16)],
        compiler_params=_params("arbitrary"),
        name="inproj",
    )(h, g, w_bf, q_gain, k_gain, ones)


def _bd2(x, m0, swap=False):
    zero = jnp.zeros_like(x)
    first = jnp.where(m0, x, zero)
    second = jnp.where(m0, zero, x)
    return jnp.concatenate([second, first] if swap else [first, second], axis=0)


def _wkv_prep_body(ps_ref, prev_ref, sp_ref, mu_ref, w0_ref, a0_ref, kk_ref, ka_ref, rk_ref,
                   wd_ref, wa_ref, wg_ref, ones_ref, tril_ref,
                   rp_ref, y0_ref, pc_ref, hinc_ref, bonus_ref, g_ref, *, dr, nd, na):
    c = pl.program_id(1)
    x = ps_ref[...]
    row0 = jnp.where(c == 0, sp_ref[...], prev_ref[7:8, :])
    rowi = lax.broadcasted_iota(jnp.int32, x.shape, 0)
    prev = jnp.where(rowi == 0, row0, pltpu.roll(x, 1, axis=0))
    xs = x + (prev - x) * mu_ref[...]
    r = xs[:, :dr]
    k = xs[:, dr:2 * dr]
    v = xs[:, 2 * dr:3 * dr]
    o = 3 * dr
    xw = xs[:, o:o + nd]
    xa = xs[:, o + nd:o + nd + na]
    xg = xs[:, o + nd + na:]

    z = w0_ref[...] + _mm3(jnp.tanh(xw), wd_ref[...])
    softplus = jnp.maximum(-z, 0.0) + jnp.log1p(jnp.exp(-jnp.abs(z)))
    ld = -jnp.exp(-softplus - 0.5)
    a = jax.nn.sigmoid(a0_ref[...] + _mm3(xa, wa_ref[...]))
    g_ref[...] = _mm(jax.nn.sigmoid(xg), wg_ref[...]).astype(BF16)

    ones = ones_ref[...]
    kk = k * kk_ref[...]
    kk = kk / jnp.maximum(jnp.sqrt(_segsum(kk * kk, ones)), 1e-12)
    k2 = k * (1.0 + (a - 1.0) * ka_ref[...])
    bonus_ref[...] = (_segsum(r * k2 * rk_ref[...], ones) * v).astype(BF16)
    av = -kk
    bv = kk * a

    cum = _mm_exact_lhs(tril_ref[...], ld)
    e_neg = jnp.exp(-cum)
    at_all = av * jnp.exp(cum - ld)
    rt_all = r * jnp.exp(cum)
    bt_all = bv * e_neg
    kt_all = k2 * e_neg

    lane = lax.broadcasted_iota(jnp.int32, (CHUNK, PAIR), 1)
    rr = lax.broadcasted_iota(jnp.int32, (CHUNK, PAIR), 0)
    m0 = lane < HEAD
    lane_in = jnp.where(m0, lane, lane - HEAD)
    strict = rr > lane_in
    incl = rr >= lane_in
    diag = rr == lane_in
    eye2 = jnp.where(diag, 1.0, 0.0).astype(F32)
    m0w = lax.broadcasted_iota(jnp.int32, (2 * CHUNK, PAIR), 1) < HEAD
    zero = jnp.zeros((CHUNK, PAIR), F32)
    zero2 = jnp.zeros((2 * CHUNK, PAIR), F32)

    bd = lambda t, swap=False: _bd2(t, m0, swap).astype(BF16)

    chains = [(slice(ch * CHUNK, (ch + 1) * CHUNK), slice(j * PAIR, (j + 1) * PAIR), (ch + 1) * CHUNK - 1)
              for ch in range(x.shape[0] // CHUNK) for j in range(dr // PAIR)]
    atp = [at_all[rs, sl] for rs, sl, _ in chains]
    rtp = [rt_all[rs, sl] for rs, sl, _ in chains]
    vp = [v[rs, sl] for rs, sl, _ in chains]
    g0, g1 = [], []
    for i, (rs, sl, _) in enumerate(chains):
        btp, ktp = bt_all[rs, sl], kt_all[rs, sl]
        lhs = jnp.concatenate([atp[i], rtp[i]], axis=0)
        g0.append(_mm(jnp.where(m0w, lhs, zero2), jnp.concatenate([btp, ktp], axis=0), NT))
        g1.append(_mm(jnp.where(m0w, zero2, lhs), jnp.concatenate([ktp, btp], axis=0), NT))
    qp = [jnp.where(strict, jnp.where(m0, a[:CHUNK], b[:CHUNK]), zero) for a, b in zip(g0, g1)]
    akmk = [jnp.concatenate([jnp.where(strict, jnp.where(m0, b[:CHUNK], a[:CHUNK]), zero),
                             jnp.where(incl, jnp.where(m0, b[CHUNK:], a[CHUNK:]), zero)], axis=0)
            for a, b in zip(g0, g1)]
    mb = [jnp.where(incl, jnp.where(m0, a[CHUNK:], b[CHUNK:]), zero) for a, b in zip(g0, g1)]
    kv = [_mm(l, bd(t, True)) for l, t in zip(akmk, vp)]

    tm = [eye2 + t for t in qp]
    bq = [bd(t) for t in qp]
    for _ in range(5):
        qp = [_mm(t, b) for t, b in zip(qp, bq)]
        bq = [bd(t) for t in qp]
        tm = [t + _mm(t, b) for t, b in zip(tm, bq)]

    au = [_mm(t, jnp.concatenate([bd(a), bd(k[:CHUNK])], axis=1)) for t, a, k in zip(tm, atp, kv)]
    ry = [_mm(t, jnp.concatenate([bd(a[:, :PAIR]), bd(a[:, PAIR:])], axis=1)) for t, a in zip(mb, au)]
    ph = []
    for i, (rs, sl, last) in enumerate(chains):
        e_end = jnp.exp(cum[last:last + 1, sl] - cum[rs, sl])
        rhs = jnp.concatenate([au[i], jnp.concatenate([zero, vp[i]], axis=1)], axis=0)
        ph.append(_mm(jnp.concatenate([bv[rs, sl] * e_end, k2[rs, sl] * e_end], axis=0), rhs, TN))
    for i, (rs, sl, last) in enumerate(chains):
        rp_ref[rs, sl] = (rtp[i] + ry[i][:, :PAIR]).astype(BF16)
        y0_ref[rs, sl] = kv[i][CHUNK:] + ry[i][:, PAIR:]
        g_c = jnp.exp(cum[last:last + 1, sl])
        pc_ref[rs, sl] = (jnp.where(m0, ph[i][:CHUNK, :PAIR], ph[i][CHUNK:, :PAIR])
                          + jnp.where(diag, g_c, 0.0)).astype(BF16)
        hinc_ref[rs, sl] = jnp.where(m0, ph[i][:CHUNK, PAIR:], ph[i][CHUNK:, PAIR:])


def _wkv_prep(ps, shift_prev, lp, ones_pair, batch, seq):
    n, ns = ps.shape
    dr = lp["w0"].shape[-1]
    nd = lp["w_decay"].shape[0]
    na = lp["w_aaa"].shape[0]
    rows = min(PREP_ROWS, seq)
    nc = seq // rows
    blk = rows // 8
    tril = jnp.kron(jnp.eye(rows // CHUNK, dtype=F32), jnp.tril(jnp.ones((CHUNK, CHUNK), F32))).astype(BF16)
    row = lambda w: pl.BlockSpec((rows, w), lambda b, c: (b * nc + c, 0))
    cs = lambda a: _const_spec(a.shape, 2)
    out = lambda dt: jax.ShapeDtypeStruct((n, dr), dt)
    consts = [lp["mu"], lp["w0"], lp["a0"], lp["k_k"], lp["k_a"], lp["r_k"],
              lp["w_decay"], lp["w_aaa"], lp["w_gate"], ones_pair, tril]
    return pl.pallas_call(
        functools.partial(_wkv_prep_body, dr=dr, nd=nd, na=na),
        grid=(batch, nc),
        in_specs=[row(ns),
                  pl.BlockSpec((8, ns), lambda b, c: (jnp.maximum((b * nc + c) * blk - 1, 0), 0)),
                  pl.BlockSpec((None, 1, ns), lambda b, c: (b, 0, 0))] + [cs(a) for a in consts],
        out_specs=[row(dr)] * 6,
        out_shape=[out(BF16), out(F32), out(BF16), out(F32), out(BF16), out(BF16)],
        compiler_params=_params("arbitrary", "arbitrary"),
        name="wkv_prep",
    )(ps, ps, shift_prev, *consts)


def _wkv_scan_body(rp_ref, y0_ref, pc_ref, hinc_ref, bonus_ref, g_ref, h0_ref, gnw_ref, gnb_ref, ones_ref,
                   y_ref, hout_ref, h_scr, y_scr, *, dr):
    c = pl.program_id(1)

    @pl.when(c == 0)
    def _():
        h_scr[...] = h0_ref[...]

    m0 = lax.broadcasted_iota(jnp.int32, (CHUNK, PAIR), 1) < HEAD
    for ch in range(rp_ref.shape[0] // CHUNK):
        rs = slice(ch * CHUNK, (ch + 1) * CHUNK)
        for j in range(dr // PAIR):
            sl = slice(j * PAIR, (j + 1) * PAIR)
            lhs = jnp.concatenate([rp_ref[rs, sl], pc_ref[rs, sl]], axis=0)
            out = _mm(lhs, _bd2(h_scr[:, sl], m0))
            y_scr[rs, sl] = y0_ref[rs, sl] + out[:CHUNK]
            h_scr[:, sl] = hinc_ref[rs, sl] + out[CHUNK:]
    hout_ref[...] = h_scr[...]

    y = y_scr[...]
    ones = ones_ref[...]
    mean = _segsum(y, ones) * (1.0 / HEAD)
    d = y - mean
    var = _segsum(d * d, ones) * (1.0 / HEAD)
    yn = d * lax.rsqrt(var + GN_EPS) * gnw_ref[...] + gnb_ref[...]
    y_ref[...] = ((yn + bonus_ref[...]) * g_ref[...]).astype(BF16)


def _wkv_scan(rp, y0, pc, hinc, bonus, g, h0, gn_w, gn_b, ones_pair, batch, seq):
    n, dr = rp.shape
    rows = min(SCAN_ROWS, seq)
    nc = seq // rows
    row = pl.BlockSpec((rows, dr), lambda b, c: (b * nc + c, 0))
    st = pl.BlockSpec((None, HEAD, dr), lambda b, c: (b, 0, 0))
    return pl.pallas_call(
        functools.partial(_wkv_scan_body, dr=dr),
        grid=(batch, nc),
        in_specs=[row] * 6 + [st, _const_spec((1, dr), 2), _const_spec((1, dr), 2),
                              _const_spec(ones_pair.shape, 2)],
        out_specs=[row, st],
        out_shape=[jax.ShapeDtypeStruct((n, dr), BF16), jax.ShapeDtypeStruct((batch, HEAD, dr), F32)],
        scratch_shapes=[pltpu.VMEM((HEAD, dr), F32), pltpu.VMEM((rows, dr), F32)],
        compiler_params=_params("arbitrary", "arbitrary"),
        name="wkv_scan",
    )(rp, y0, pc, hinc, bonus, g, h0, gn_w, gn_b, ones_pair)


def _flash_init(m_scr, acc_scr):
    m_scr[...] = jnp.full(m_scr.shape, NEG_BIG, F32)
    acc_scr[...] = jnp.zeros(acc_scr.shape, F32)


def _lane_rep(x, width):
    if width % PAIR == 0:
        return x if width == PAIR else jnp.concatenate([x] * (width // PAIR), axis=1)
    return x[:, :width]


def _flash_update(q, k_of, v_of, tk, m_scr, acc_scr, mask=None):
    tq = q.shape[0]
    first = lax.broadcasted_iota(jnp.int32, (tq, PAIR), 1) < HEAD
    zq = jnp.zeros((tq, PAIR), BF16)
    ones = jnp.ones((tk, PAIR), BF16)
    n_maps = 2 * (q.shape[1] // PAIR)

    def scores(i):
        qh = q[:, (i // 2) * PAIR:(i // 2 + 1) * PAIR]
        qm = jnp.where(first, qh, zq) if i % 2 == 0 else jnp.where(first, zq, qh)
        return _dg(qm, k_of(i // 2), NT)

    s_next = scores(0)
    for i in range(n_maps):
        s = s_next
        if i + 1 < n_maps:
            s_next = scores(i + 1)
        if mask is not None:
            s = jnp.where(mask, s, NEG_BIG)
        v_aug = jnp.concatenate([v_of(i // 2), ones], axis=1)
        m_prev = m_scr[i]
        m_new = jnp.maximum(m_prev, jnp.max(s, axis=1, keepdims=True))
        alpha = jnp.exp2(m_prev - m_new)
        p = jnp.exp2((s - _lane_rep(m_new, tk)).astype(BF16))
        acc_scr[i] = _lane_rep(alpha, 2 * PAIR) * acc_scr[i] + _dg(p, v_aug, NN)
        m_scr[i] = m_new


def _flash_finish(lam, sub_ref, scale, y_ref, acc_scr):
    for h in range(y_ref.shape[1] // PAIR):
        a1, a2 = acc_scr[2 * h], acc_scr[2 * h + 1]
        o = a1[:, :PAIR] / a1[:, PAIR:] - lam * (a2[:, :PAIR] / a2[:, PAIR:])
        ms = jnp.mean(o * o, axis=-1, keepdims=True)
        y = o * lax.rsqrt(ms + NORM_EPS) * sub_ref[...] * scale
        y_ref[:, h * PAIR:(h + 1) * PAIR] = y.astype(BF16)


def _attn_prompt_body(qt_ref, kt_ref, lam_ref, q_ref, k_ref, v_ref, sub_ref, y_ref, m_scr, acc_scr, *, tq, scale):
    s = pl.program_id(1)
    qi = qt_ref[s]
    kj = kt_ref[s]
    head = lambda h: slice(h * PAIR, (h + 1) * PAIR)
    k_of = lambda h: k_ref[:, head(h)]
    v_of = lambda h: v_ref[:, head(h)]

    @pl.when(kj == 0)
    def _():
        _flash_init(m_scr, acc_scr)

    @pl.when(kj < qi)
    def _():
        _flash_update(q_ref[...], k_of, v_of, tq, m_scr, acc_scr)

    @pl.when(kj == qi)
    def _():
        qrow = lax.broadcasted_iota(jnp.int32, (tq, tq), 0)
        kcol = lax.broadcasted_iota(jnp.int32, (tq, tq), 1)
        mask = (kcol // CHUNK) <= (qrow // CHUNK)
        _flash_update(q_ref[...], k_of, v_of, tq, m_scr, acc_scr, mask)
        _flash_finish(lam_ref[0], sub_ref, scale, y_ref, acc_scr)


def _attn_prompt(lam, q, kb, vb, subln, scale, batch, seq):
    n, dd = q.shape
    tq = min(512, seq)
    nq = seq // tq
    nh2 = 2 * dd // PAIR
    pairs = [(i, j) for i in range(nq) for j in range(i + 1)]
    qt = jnp.array([p[0] for p in pairs], jnp.int32)
    kt = jnp.array([p[1] for p in pairs], jnp.int32)
    kv_spec = pl.BlockSpec((tq, dd), lambda b, s, qt, kt: (b * nq + kt[s], 0))
    q_spec = pl.BlockSpec((tq, dd), lambda b, s, qt, kt: (b * nq + qt[s], 0))
    return pl.pallas_call(
        functools.partial(_attn_prompt_body, tq=tq, scale=scale),
        grid_spec=pltpu.PrefetchScalarGridSpec(
            num_scalar_prefetch=2,
            grid=(batch, len(pairs)),
            in_specs=[pl.BlockSpec(memory_space=pltpu.SMEM), q_spec, kv_spec, kv_spec,
                      pl.BlockSpec((1, PAIR), lambda b, s, qt, kt: (0, 0))],
            out_specs=q_spec,
            scratch_shapes=[pltpu.VMEM((nh2, tq, PAIR), F32), pltpu.VMEM((nh2, tq, 2 * PAIR), F32)]),
        out_shape=jax.ShapeDtypeStruct((n, dd), BF16),
        compiler_params=_params("arbitrary", "arbitrary"),
        name="attn_prompt",
    )(qt, kt, lam, q, kb, vb, subln)


def _attn_sample_body(lam_ref, q_ref, ck_ref, cv_ref, k_ref, v_ref, sub_ref, y_ref, m_scr, acc_scr, *, scale):
    kj = pl.program_id(1)
    last = pl.num_programs(1) - 1

    @pl.when(kj == 0)
    def _():
        _flash_init(m_scr, acc_scr)

    head = lambda h: slice(h * PAIR, (h + 1) * PAIR)

    nh = q_ref.shape[1] // PAIR
    tk = ck_ref.shape[0] // nh

    @pl.when(kj < last)
    def _():
        _flash_update(q_ref[...], lambda h: ck_ref[pl.ds(h, tk, stride=nh), :].astype(BF16),
                      lambda h: cv_ref[pl.ds(h, tk, stride=nh), :].astype(BF16), tk, m_scr, acc_scr)

    @pl.when(kj == last)
    def _():
        _flash_update(q_ref[...], lambda h: k_ref[:, head(h)], lambda h: v_ref[:, head(h)],
                      k_ref.shape[0], m_scr, acc_scr)
        _flash_finish(lam_ref[0], sub_ref, scale, y_ref, acc_scr)


def _attn_sample(lam, q, cache_k, cache_v, layer, kb, vb, subln, scale, batch, seq):
    n, dd = q.shape
    past = cache_k.shape[2]
    tk = min(1024, past)
    nk = past // tk
    nh2 = 2 * dd // PAIR
    row = pl.BlockSpec((seq, dd), lambda b, j: (b, 0))
    nh = dd // PAIR
    cache_k = cache_k.reshape(cache_k.shape[:2] + (past * nh, PAIR))
    cache_v = cache_v.reshape(cache_v.shape[:2] + (past * nh, PAIR))
    cache = pl.BlockSpec((None, None, tk * nh, PAIR), lambda b, j: (layer, b, jnp.minimum(j, nk - 1), 0))
    return pl.pallas_call(
        functools.partial(_attn_sample_body, scale=scale),
        grid=(batch, nk + 1),
        in_specs=[pl.BlockSpec(memory_space=pltpu.SMEM), row, cache, cache, row, row, _const_spec((1, PAIR), 2)],
        out_specs=row,
        out_shape=jax.ShapeDtypeStruct((n, dd), BF16),
        scratch_shapes=[pltpu.VMEM((nh2, seq, PAIR), F32), pltpu.VMEM((nh2, seq, 2 * PAIR), F32)],
        compiler_params=_params("arbitrary", "arbitrary"),
        name="attn_sample",
    )(lam, q, cache_k, cache_v, kb, vb, subln)


def _ffn_body(h_ref, yr_ref, ya_ref, wo_ref, gf_ref, wup_ref, cw_ref, cb_ref, wdn_ref, cprev_ref,
              hout_ref, cnew_ref, carry_scr, *, dff, fc):
    t = pl.program_id(1)

    @pl.when(t == 0)
    def _():
        carry_scr[...] = cprev_ref[...]

    y = jnp.concatenate([yr_ref[...], ya_ref[...]], axis=1)
    h1 = h_ref[...] + _dg(y, wo_ref[...], NN)
    ms = jnp.mean(h1 * h1, axis=-1, keepdims=True)
    xn = (h1 * lax.rsqrt(ms + NORM_EPS) * gf_ref[...]).astype(BF16)
    tm = h1.shape[0]
    rowi = lax.broadcasted_iota(jnp.int32, (tm, fc), 0)
    acc = jnp.zeros(h1.shape, F32)

    def up(c):
        return (_dg(xn, wup_ref[:, c * fc:(c + 1) * fc], NN),
                _dg(xn, wup_ref[:, dff + c * fc:dff + (c + 1) * fc], NN))

    nxt = up(0)
    for c in range(dff // fc):
        sl = slice(c * fc, (c + 1) * fc)
        gt, u = nxt
        if c + 1 < dff // fc:
            nxt = up(c + 1)
        p1 = carry_scr[7:8, sl]
        p2 = carry_scr[6:7, sl]
        g1 = jnp.where(rowi == 0, p1, pltpu.roll(gt, 1, axis=0))
        g2 = jnp.where(rowi == 0, p2, jnp.where(rowi == 1, p1, pltpu.roll(gt, 2, axis=0)))
        gc = cb_ref[:, sl] + g2 * cw_ref[0:1, sl] + g1 * cw_ref[1:2, sl] + gt * cw_ref[2:3, sl]
        carry_scr[:, sl] = gt[tm - 8:, :]
        hh = (gc * jax.nn.sigmoid(gc) * u).astype(BF16)
        acc = acc + _dg(hh, wdn_ref[sl, :], NN)
    hout_ref[...] = h1 + acc
    cnew_ref[...] = carry_scr[...]


def _ffn(h, yr, ya, wo, g_ffn, wup, cw, cb, wdn, conv_prev8, batch, seq):
    n, d = h.shape
    dff = cb.shape[-1]
    fc = 256
    tm = min(512, seq)
    nt = seq // tm
    row = lambda w: pl.BlockSpec((tm, w), lambda b, t: (b * nt + t, 0))
    st = pl.BlockSpec((None, 8, dff), lambda b, t: (b, 0, 0))
    cs = lambda a: _const_spec(a.shape, 2)
    return pl.pallas_call(
        functools.partial(_ffn_body, dff=dff, fc=fc),
        grid=(batch, nt),
        in_specs=[row(d), row(yr.shape[1]), row(ya.shape[1]), cs(wo), cs(g_ffn), cs(wup), cs(cw), cs(cb), cs(wdn), st],
        out_specs=[row(d), st],
        out_shape=[jax.ShapeDtypeStruct((n, d), F32), jax.ShapeDtypeStruct((batch, 8, dff), F32)],
        scratch_shapes=[pltpu.VMEM((8, dff), F32)],
        compiler_params=_params("arbitrary", "arbitrary"),
        name="ffn",
    )(h, yr, ya, wo, g_ffn, wup, cw, cb, wdn, conv_prev8)


def _stack_heads_body(*refs, depth):
    layer = pl.program_id(0)
    o_ref = refs[depth]
    for i in range(depth):
        @pl.when(layer == i)
        def _(i=i):
            for h in range(o_ref.shape[1]):
                o_ref[:, h, :] = refs[i][:, h * PAIR:(h + 1) * PAIR]


def _stack_heads(xs):
    depth = len(xs)
    n, dd = xs[0].shape
    nh = dd // PAIR
    tm = min(1024, n)

    def in_spec(i):
        return pl.BlockSpec((tm, dd), lambda l, t: (jnp.where(l == i, t, 0), 0))

    return pl.pallas_call(
        functools.partial(_stack_heads_body, depth=depth),
        grid=(depth, n // tm),
        in_specs=[in_spec(i) for i in range(depth)],
        out_specs=pl.BlockSpec((None, tm, nh, PAIR), lambda l, t: (l, t, 0, 0)),
        out_shape=jax.ShapeDtypeStruct((depth, n, nh, PAIR), xs[0].dtype),
        compiler_params=_params("arbitrary", "arbitrary"),
        name="stack_heads",
    )(*xs)


PREP_ROWS = 256
SCAN_ROWS = 512


def _lambda_init(layer):
    return 0.8 - 0.6 * math.exp(-0.3 * layer)


def _run_group(x, depth, layers, ones_pair, shift_prev, wkv_prev, conv_prev, attend):
    batch, seq, d = x.shape
    h = x.reshape(batch * seq, d)
    ks, vs, ws, ss, cs = [], [], [], [], []
    for l in range(depth):
        lp = layers[l]
        ns = lp["mu"].shape[-1]
        dd = lp["q_gain"].shape[-1]
        dr = lp["w0"].shape[-1]
        dff = lp["conv_b"].shape[-1]
        ps, q, k, v, kb, vb = _inproj(h, lp["g_mix"], lp["w_in"], lp["q_gain"], lp["k_gain"], ns, dd)

        sp = jnp.zeros((batch, 1, ns), F32) if shift_prev is None else shift_prev[l][:, None, :]
        if wkv_prev is None:
            h0 = jnp.zeros((batch, HEAD, dr), F32)
        else:
            h0 = jnp.transpose(wkv_prev[l], (0, 3, 1, 2)).reshape(batch, HEAD, dr)
        rp, y0, pc, hinc, bonus, g = _wkv_prep(ps, sp, lp, ones_pair, batch, seq)
        yr, hout = _wkv_scan(rp, y0, pc, hinc, bonus, g, h0, lp["gn_w"], lp["gn_b"], ones_pair, batch, seq)

        lam_init = _lambda_init(l)
        ya = attend(l, lp["lam"], q, kb, vb, lp["subln"], 1.0 - lam_init, batch, seq)

        cp = jnp.zeros((batch, 8, dff), F32) if conv_prev is None else jnp.pad(conv_prev[l], ((0, 0), (6, 0), (0, 0)))
        h, cnew = _ffn(h, yr, ya, lp["w_out"], lp["g_ffn"], lp["w_up"], lp["conv_w"], lp["conv_b"], lp["w_down"],
                       cp, batch, seq)

        ks.append(k)
        vs.append(v)
        ws.append(jnp.transpose(hout.reshape(batch, HEAD, dr // HEAD, HEAD), (0, 2, 3, 1)))
        ss.append(ps.reshape(batch, seq, ns)[:, -1])
        cs.append(cnew[:, 6:8])
    heads = lambda xs: _stack_heads(xs).reshape(depth, batch, seq, -1, PAIR)
    return (h.reshape(batch, seq, d), heads(ks), heads(vs), jnp.stack(ws), jnp.stack(ss), jnp.stack(cs))


def kernel(x_prompt, x_sample, cache_k, cache_v, state_wkv, state_shift, state_conv, g_mix, w_in, mu_shift, w0, w_decay, a0, w_aaa, w_gate, k_k, k_a, r_k, gn_w, gn_b, q_gain, k_gain, lambdas, subln_gain, w_out, g_ffn, w_ffn_in, conv_w, conv_b, w_ffn_out):
    depth = w_in.shape[0]
    dr = w0.shape[-1]
    ns = mu_shift.shape[-1]
    dd = (w_in.shape[-1] - ns) // 3
    assert dr % PAIR == 0 and dd % PAIR == 0 and q_gain.shape[-1] == HEAD and subln_gain.shape[-1] == PAIR
    assert x_prompt.shape[1] % CHUNK == 0 and x_sample.shape[1] == CHUNK

    row = lambda a: a.reshape(1, -1).astype(F32)
    layers = []
    for l in range(depth):
        lv = lambdas[l].astype(F32)
        lam = jnp.exp(jnp.sum(lv[0] * lv[1])) - jnp.exp(jnp.sum(lv[2] * lv[3])) + _lambda_init(l)
        layers.append(dict(
            g_mix=row(g_mix[l]), w_in=w_in[l].astype(BF16), mu=row(mu_shift[l]), w0=row(w0[l]), a0=row(a0[l]),
            w_decay=w_decay[l], w_aaa=w_aaa[l], w_gate=w_gate[l], k_k=row(k_k[l]), k_a=row(k_a[l]), r_k=row(r_k[l]),
            gn_w=row(gn_w[l]), gn_b=row(gn_b[l]),
            q_gain=row(jnp.tile(q_gain[l], dd // HEAD)), k_gain=row(jnp.tile(k_gain[l], dd // HEAD)),
            lam=lam.reshape(1), subln=row(subln_gain[l]), w_out=w_out[l].astype(BF16), g_ffn=row(g_ffn[l]),
            w_up=w_ffn_in[l].astype(BF16), conv_w=conv_w[l], conv_b=row(conv_b[l]), w_down=w_ffn_out[l].astype(BF16)))

    ones_pair = jnp.kron(jnp.eye(2, dtype=F32), jnp.ones((HEAD, HEAD), F32)).astype(BF16)

    def attend_prompt(l, lam, q, kb, vb, subln, scale, batch, seq):
        return _attn_prompt(lam, q, kb, vb, subln, scale, batch, seq)


    def attend_sample(l, lam, q, kb, vb, subln, scale, batch, seq):
        return _attn_sample(lam, q, cache_k, cache_v, l, kb, vb, subln, scale, batch, seq)

    yp, pk, pv, pw, ps_, pc = _run_group(x_prompt, depth, layers, ones_pair, None, None, None, attend_prompt)
    ys, sk, sv, sw, ss, sc = _run_group(x_sample, depth, layers, ones_pair, state_shift, state_wkv, state_conv,
                                        attend_sample)
    return (yp, ys, pk, pv, pw, ps_, pc, sk, sv, sw, ss, sc)
```

```python
import functools
import math

import jax
import jax.numpy as jnp
from jax import lax
from jax.experimental import pallas as pl
from jax.experimental.pallas import tpu as pltpu

F32 = jnp.float32
BF16 = jnp.bfloat16

HEAD = 64
PAIR = 2 * HEAD
CHUNK = 64
NORM_EPS = 1e-6
GN_EPS = 64e-5
NEG_BIG = -1e30
LOG2E = math.log2(math.e)
VMEM_LIMIT = 56 * 1024 * 1024

NN = (((1,), (0,)), ((), ()))
NT = (((1,), (1,)), ((), ()))
TN = (((0,), (0,)), ((), ()))


def _dg(a, b, dn):
    return lax.dot_general(a, b, dn, preferred_element_type=F32)


def _split2(x):
    hi = x.astype(BF16)
    lo = (x - hi.astype(F32)).astype(BF16)
    return hi, lo


def _mm(a, b, dn=NN):
    return _dg(a.astype(BF16), b.astype(BF16), dn)


def _mm3(a, b):
    ah, al = _split2(a)
    bh, bl = _split2(b)
    return _dg(ah, bh, NN) + (_dg(ah, bl, NN) + _dg(al, bh, NN))


def _mm_exact_rhs(a, e):
    hi, lo = _split2(a)
    return _dg(hi, e, NN) + _dg(lo, e, NN)


def _mm_exact_lhs(e, b):
    hi, lo = _split2(b)
    return _dg(e, hi, NN) + _dg(e, lo, NN)


def _segsum(x, ones_bd):
    w = ones_bd.shape[0]
    parts = [_mm_exact_rhs(x[:, i:i + w], ones_bd) for i in range(0, x.shape[1], w)]
    return parts[0] if len(parts) == 1 else jnp.concatenate(parts, axis=1)


def _params(*sem):
    return pltpu.CompilerParams(dimension_semantics=sem, vmem_limit_bytes=VMEM_LIMIT)


def _const_spec(shape, grid_rank):
    zeros = (0,) * len(shape)
    if grid_rank == 1:
        return pl.BlockSpec(shape, lambda i: zeros, pipeline_mode=pl.Buffered(1))
    if grid_rank == 2:
        return pl.BlockSpec(shape, lambda i, j: zeros, pipeline_mode=pl.Buffered(1))
    return pl.BlockSpec(shape, lambda i, j, k: zeros, pipeline_mode=pl.Buffered(1))


def _inproj_body(x_ref, g_ref, w_ref, qg_ref, kg_ref, ones_ref,
                 ps_ref, q_ref, k_ref, v_ref, kb_ref, vb_ref, *, ns, dd):
    x = x_ref[...]
    ms = jnp.mean(x * x, axis=-1, keepdims=True)
    xn = (x * lax.rsqrt(ms + NORM_EPS) * g_ref[...]).astype(BF16)
    ps_ref[...] = _dg(xn, w_ref[:, :ns], NN)
    ones = ones_ref[...]

    def head_norm(t, gain):
        w = ones.shape[0]
        sq = (t * t).astype(BF16)
        ss = jnp.concatenate([_dg(sq[:, i:i + w], ones, NN) for i in range(0, dd, w)], axis=1)
        return t * lax.rsqrt(ss * (1.0 / HEAD) + NORM_EPS) * gain

    q = _dg(xn, w_ref[:, ns:ns + dd], NN)
    k = _dg(xn, w_ref[:, ns + dd:ns + 2 * dd], NN)
    v = _dg(xn, w_ref[:, ns + 2 * dd:ns + 3 * dd], NN)
    v_ref[...] = v
    vb_ref[...] = v.astype(BF16)
    q = head_norm(q, qg_ref[...])
    q_ref[...] = (q * (HEAD ** -0.5 * LOG2E)).astype(BF16)
    k = head_norm(k, kg_ref[...])
    k_ref[...] = k
    kb_ref[...] = k.astype(BF16)


def _inproj(h, g, w_bf, q_gain, k_gain, ns, dd):
    n, d = h.shape
    tm = min(512, n)
    ones = jnp.kron(jnp.eye(2 * PAIR // HEAD, dtype=F32), jnp.ones((HEAD, HEAD), F32)).astype(BF16)
    row = lambda w: pl.BlockSpec((tm, w), lambda i: (i, 0))
    return pl.pallas_call(
        functools.partial(_inproj_body, ns=ns, dd=dd),
        grid=(n // tm,),
        in_specs=[row(d), _const_spec((1, d), 1), _const_spec(w_bf.shape, 1),
                  _const_spec((1, dd), 1), _const_spec((1, dd), 1), _const_spec(ones.shape, 1)],
        out_specs=[row(ns), row(dd), row(dd), row(dd), row(dd), row(dd)],
        out_shape=[jax.ShapeDtypeStruct((n, ns), F32), jax.ShapeDtypeStruct((n, dd), BF16),
                   jax.ShapeDtypeStruct((n, dd), F32), jax.ShapeDtypeStruct((n, dd), F32),
                   jax.ShapeDtypeStruct((n, dd), BF16), jax.ShapeDtypeStruct((n, dd), BF16)],
        compiler_params=_params("arbitrary"),
        name="inproj",
    )(h, g, w_bf, q_gain, k_gain, ones)


def _bd2(x, m0, swap=False):
    zero = jnp.zeros_like(x)
    first = jnp.where(m0, x, zero)
    second = jnp.where(m0, zero, x)
    return jnp.concatenate([second, first] if swap else [first, second], axis=0)


def _wkv_prep_body(ps_ref, prev_ref, sp_ref, mu_ref, w0_ref, a0_ref, kk_ref, ka_ref, rk_ref,
                   wd_ref, wa_ref, wg_ref, ones_ref, tril_ref,
                   rp_ref, y0_ref, pc_ref, hinc_ref, bonus_ref, g_ref, *, dr, nd, na, sub):
    c = pl.program_id(1)
    ns = ps_ref.shape[1]
    ones = ones_ref[...]
    sigmoid = lambda t: 0.5 + 0.5 * jnp.tanh(0.5 * t)
    seg = lambda t: jnp.concatenate([_dg(t[:, i:i + PAIR].astype(BF16), ones, NN) for i in range(0, dr, PAIR)],
                                    axis=1)

    lane = lax.broadcasted_iota(jnp.int32, (CHUNK, PAIR), 1)
    rr = lax.broadcasted_iota(jnp.int32, (CHUNK, PAIR), 0)
    m0 = lane < HEAD
    lane_in = jnp.where(m0, lane, lane - HEAD)
    strict = rr > lane_in
    incl = rr >= lane_in
    diag = rr == lane_in
    eye2 = jnp.where(diag, 1.0, 0.0).astype(F32)
    m0w = lax.broadcasted_iota(jnp.int32, (2 * CHUNK, PAIR), 1) < HEAD
    zero = jnp.zeros((CHUNK, PAIR), F32)
    zero2 = jnp.zeros((2 * CHUNK, PAIR), F32)

    bd = lambda t, swap=False: _bd2(t, m0, swap).astype(BF16)

    def prologue(i, pro):
        rows = slice(i * sub, (i + 1) * sub)
        x = ps_ref[rows, :]
        if i == 0:
            row0 = jnp.where(c == 0, sp_ref[...], prev_ref[7:8, :])
        else:
            row0 = ps_ref[i * sub - 1:i * sub, :]
        rowi = lax.broadcasted_iota(jnp.int32, (sub, ns), 0)
        prev = jnp.where(rowi == 0, row0, pltpu.roll(x, 1, axis=0))
        xs = x + (prev - x) * mu_ref[...]
        r = xs[:, :dr]
        k = xs[:, dr:2 * dr]
        v = xs[:, 2 * dr:3 * dr]
        o = 3 * dr
        t_w = jnp.tanh(xs[:, o:o + nd])
        xa = xs[:, o + nd:o + nd + na]
        s_g = sigmoid(xs[:, o + nd + na:])
        yield
        z = _mm(t_w, wd_ref[...])
        al = _mm(xa, wa_ref[...])
        g_ref[rows, :] = _mm(s_g, wg_ref[...]).astype(BF16)
        yield
        ld = -math.exp(-0.5) * sigmoid(w0_ref[...] + z)
        a = sigmoid(a0_ref[...] + al)
        kk = k * kk_ref[...]
        k2 = k * (1.0 + (a - 1.0) * ka_ref[...])
        kk_ss = seg(kk * kk)
        bonus = seg(r * k2 * rk_ref[...])
        cum = _mm_exact_lhs(tril_ref[...], ld)
        yield
        kk = kk * lax.rsqrt(jnp.maximum(kk_ss, 1e-24))
        bonus_ref[rows, :] = (bonus * v).astype(BF16)
        bv = kk * a
        e_cum = jnp.exp(cum)
        e_neg = jnp.exp(-cum)
        chunk_start = lax.broadcasted_iota(jnp.int32, cum.shape, 0) % CHUNK == 0
        e_prev = jnp.where(chunk_start, 1.0, pltpu.roll(e_cum, 1, axis=0))
        pro.update(at=-kk * e_prev, rt=r * e_cum, bt=bv * e_neg, kt=k2 * e_neg, bv=bv, k2=k2, v=v,
                   e_cum=e_cum, e_neg=e_neg)

    def algebra(i, pro):
        chains = [(slice(ch * CHUNK, (ch + 1) * CHUNK), slice(j * PAIR, (j + 1) * PAIR), (ch + 1) * CHUNK - 1)
                  for ch in range(sub // CHUNK) for j in range(dr // PAIR)]
        atp = [pro["at"][rs, sl] for rs, sl, _ in chains]
        rtp = [pro["rt"][rs, sl] for rs, sl, _ in chains]
        vp = [pro["v"][rs, sl] for rs, sl, _ in chains]
        g0, g1 = [], []
        for n, (rs, sl, _) in enumerate(chains):
            btp, ktp = pro["bt"][rs, sl], pro["kt"][rs, sl]
            lhs = jnp.concatenate([atp[n], rtp[n]], axis=0)
            g0.append(_mm(jnp.where(m0w, lhs, zero2), jnp.concatenate([btp, ktp], axis=0), NT))
            g1.append(_mm(jnp.where(m0w, zero2, lhs), jnp.concatenate([ktp, btp], axis=0), NT))
        yield
        qp = [jnp.where(strict, jnp.where(m0, a[:CHUNK], b[:CHUNK]), zero) for a, b in zip(g0, g1)]
        akmk = [jnp.concatenate([jnp.where(strict, jnp.where(m0, b[:CHUNK], a[:CHUNK]), zero),
                                 jnp.where(incl, jnp.where(m0, b[CHUNK:], a[CHUNK:]), zero)], axis=0)
                for a, b in zip(g0, g1)]
        mb = [jnp.where(incl, jnp.where(m0, a[CHUNK:], b[CHUNK:]), zero) for a, b in zip(g0, g1)]
        kv = [_mm(l, bd(t, True)) for l, t in zip(akmk, vp)]
        yield
        tm = [eye2 + t for t in qp]
        bq = [bd(t) for t in qp]
        for _ in range(5):
            qp = [_mm(t, b) for t, b in zip(qp, bq)]
            yield
            bq = [bd(t) for t in qp]
            tm = [t + _mm(t, b) for t, b in zip(tm, bq)]
            yield
        au = [_mm(t, jnp.concatenate([bd(a), bd(k[:CHUNK])], axis=1)) for t, a, k in zip(tm, atp, kv)]
        yield
        ry = [_mm(t, jnp.concatenate([bd(a[:, :PAIR]), bd(a[:, PAIR:])], axis=1)) for t, a in zip(mb, au)]
        ph = []
        for n, (rs, sl, last) in enumerate(chains):
            e_end = pro["e_cum"][last:last + 1, sl] * pro["e_neg"][rs, sl]
            rhs = jnp.concatenate([au[n], jnp.concatenate([zero, vp[n]], axis=1)], axis=0)
            ph.append(_mm(jnp.concatenate([pro["bv"][rs, sl] * e_end, pro["k2"][rs, sl] * e_end], axis=0), rhs, TN))
        yield
        for n, (rs, sl, last) in enumerate(chains):
            ro = slice(i * sub + rs.start, i * sub + rs.stop)
            rp_ref[ro, sl] = (rtp[n] + ry[n][:, :PAIR]).astype(BF16)
            y0_ref[ro, sl] = kv[n][CHUNK:] + ry[n][:, PAIR:]
            g_c = pro["e_cum"][last:last + 1, sl]
            pc_ref[ro, sl] = (jnp.where(m0, ph[n][:CHUNK, :PAIR], ph[n][CHUNK:, :PAIR])
                              + jnp.where(diag, g_c, 0.0)).astype(BF16)
            hinc_ref[ro, sl] = jnp.where(m0, ph[n][:CHUNK, PAIR:], ph[n][CHUNK:, PAIR:])

    nsub = ps_ref.shape[0] // sub
    pros = [dict() for _ in range(nsub)]
    for _ in prologue(0, pros[0]):
        pass
    for i in range(nsub):
        ahead = prologue(i + 1, pros[i + 1]) if i + 1 < nsub else iter(())
        for stage, _ in enumerate(algebra(i, pros[i])):
            if stage % 2 == 0:
                next(ahead, None)
        for _ in ahead:
            pass


def _wkv_prep(ps, shift_prev, lp, ones_pair, batch, seq):
    n, ns = ps.shape
    dr = lp["w0"].shape[-1]
    nd = lp["w_decay"].shape[0]
    na = lp["w_aaa"].shape[0]
    rows = min(PREP_ROWS, seq)
    sub = min(PREP_SUB, rows)
    nc = seq // rows
    blk = rows // 8
    tril = jnp.kron(jnp.eye(sub // CHUNK, dtype=F32), jnp.tril(jnp.ones((CHUNK, CHUNK), F32))).astype(BF16)
    row = lambda w: pl.BlockSpec((rows, w), lambda b, c: (b * nc + c, 0))
    cs = lambda a: _const_spec(a.shape, 2)
    out = lambda dt: jax.ShapeDtypeStruct((n, dr), dt)
    consts = [lp["mu"], lp["w0"], lp["a0"], lp["k_k"], lp["k_a"], lp["r_k"],
              lp["w_decay"], lp["w_aaa"], lp["w_gate"], ones_pair, tril]
    return pl.pallas_call(
        functools.partial(_wkv_prep_body, dr=dr, nd=nd, na=na, sub=sub),
        grid=(batch, nc),
        in_specs=[row(ns),
                  pl.BlockSpec((8, ns), lambda b, c: (jnp.maximum((b * nc + c) * blk - 1, 0), 0)),
                  pl.BlockSpec((None, 1, ns), lambda b, c: (b, 0, 0))] + [cs(a) for a in consts],
        out_specs=[row(dr)] * 6,
        out_shape=[out(BF16), out(F32), out(BF16), out(F32), out(BF16), out(BF16)],
        compiler_params=_params("arbitrary", "arbitrary"),
        name="wkv_prep",
    )(ps, ps, shift_prev, *consts)


def _wkv_scan_body(rp_ref, y0_ref, pc_ref, hinc_ref, bonus_ref, g_ref, h0_ref, gnw_ref, gnb_ref, ones_ref,
                   y_ref, hout_ref, h_scr, y_scr, *, dr):
    c = pl.program_id(1)

    @pl.when(c == 0)
    def _():
        h_scr[...] = h0_ref[...]

    m0 = lax.broadcasted_iota(jnp.int32, (CHUNK, PAIR), 1) < HEAD
    for ch in range(rp_ref.shape[0] // CHUNK):
        rs = slice(ch * CHUNK, (ch + 1) * CHUNK)
        for j in range(dr // PAIR):
            sl = slice(j * PAIR, (j + 1) * PAIR)
            lhs = jnp.concatenate([rp_ref[rs, sl], pc_ref[rs, sl]], axis=0)
            out = _mm(lhs, _bd2(h_scr[:, sl], m0))
            y_scr[rs, sl] = y0_ref[rs, sl] + out[:CHUNK]
            h_scr[:, sl] = hinc_ref[rs, sl] + out[CHUNK:]
    hout_ref[...] = h_scr[...]

    y = y_scr[...]
    ones = ones_ref[...]
    mean = _segsum(y, ones) * (1.0 / HEAD)
    d = y - mean
    var = _segsum(d * d, ones) * (1.0 / HEAD)
    yn = d * lax.rsqrt(var + GN_EPS) * gnw_ref[...] + gnb_ref[...]
    y_ref[...] = ((yn + bonus_ref[...]) * g_ref[...]).astype(BF16)


def _wkv_scan(rp, y0, pc, hinc, bonus, g, h0, gn_w, gn_b, ones_pair, batch, seq):
    n, dr = rp.shape
    rows = min(SCAN_ROWS, seq)
    nc = seq // rows
    row = pl.BlockSpec((rows, dr), lambda b, c: (b * nc + c, 0))
    st = pl.BlockSpec((None, HEAD, dr), lambda b, c: (b, 0, 0))
    return pl.pallas_call(
        functools.partial(_wkv_scan_body, dr=dr),
        grid=(batch, nc),
        in_specs=[row] * 6 + [st, _const_spec((1, dr), 2), _const_spec((1, dr), 2),
                              _const_spec(ones_pair.shape, 2)],
        out_specs=[row, st],
        out_shape=[jax.ShapeDtypeStruct((n, dr), BF16), jax.ShapeDtypeStruct((batch, HEAD, dr), F32)],
        scratch_shapes=[pltpu.VMEM((HEAD, dr), F32), pltpu.VMEM((rows, dr), F32)],
        compiler_params=_params("arbitrary", "arbitrary"),
        name="wkv_scan",
    )(rp, y0, pc, hinc, bonus, g, h0, gn_w, gn_b, ones_pair)


def _flash_init(m_scr, acc_scr):
    m_scr[...] = jnp.full(m_scr.shape, NEG_BIG, F32)
    acc_scr[...] = jnp.zeros(acc_scr.shape, F32)


def _lane_rep(x, width):
    if width % PAIR == 0:
        return x if width == PAIR else jnp.concatenate([x] * (width // PAIR), axis=1)
    return x[:, :width]


def _flash_update(q, k_of, v_of, tk, m_scr, acc_scr, mask=None):
    tq = q.shape[0]
    first = lax.broadcasted_iota(jnp.int32, (tq, PAIR), 1) < HEAD
    zq = jnp.zeros((tq, PAIR), BF16)
    ones = jnp.ones((tk, PAIR), BF16)
    n_maps = 2 * (q.shape[1] // PAIR)

    def scores(i):
        qh = q[:, (i // 2) * PAIR:(i // 2 + 1) * PAIR]
        qm = jnp.where(first, qh, zq) if i % 2 == 0 else jnp.where(first, zq, qh)
        return _dg(qm, k_of(i // 2), NT)

    s_next = scores(0)
    for i in range(n_maps):
        s = s_next
        if i + 1 < n_maps:
            s_next = scores(i + 1)
        if mask is not None:
            s = jnp.where(mask, s, NEG_BIG)
        v_aug = jnp.concatenate([v_of(i // 2), ones], axis=1)
        m_prev = m_scr[i]
        m_new = jnp.maximum(m_prev, jnp.max(s, axis=1, keepdims=True))
        alpha = jnp.exp2(m_prev - m_new)
        p = jnp.exp2((s - _lane_rep(m_new, tk)).astype(BF16))
        acc_scr[i] = _lane_rep(alpha, 2 * PAIR) * acc_scr[i] + _dg(p, v_aug, NN)
        m_scr[i] = m_new


def _flash_finish(lam, sub_ref, scale, y_ref, acc_scr):
    for h in range(y_ref.shape[1] // PAIR):
        a1, a2 = acc_scr[2 * h], acc_scr[2 * h + 1]
        o = a1[:, :PAIR] / a1[:, PAIR:] - lam * (a2[:, :PAIR] / a2[:, PAIR:])
        ms = jnp.mean(o * o, axis=-1, keepdims=True)
        y = o * lax.rsqrt(ms + NORM_EPS) * sub_ref[...] * scale
        y_ref[:, h * PAIR:(h + 1) * PAIR] = y.astype(BF16)


def _attn_prompt_body(qt_ref, kt_ref, lam_ref, q_ref, k_ref, v_ref, sub_ref, y_ref, m_scr, acc_scr, *, tq, scale):
    s = pl.program_id(1)
    qi = qt_ref[s]
    kj = kt_ref[s]
    head = lambda h: slice(h * PAIR, (h + 1) * PAIR)
    k_of = lambda h: k_ref[:, head(h)]
    v_of = lambda h: v_ref[:, head(h)]

    @pl.when(kj == 0)
    def _():
        _flash_init(m_scr, acc_scr)

    @pl.when(kj < qi)
    def _():
        _flash_update(q_ref[...], k_of, v_of, tq, m_scr, acc_scr)

    @pl.when(kj == qi)
    def _():
        qrow = lax.broadcasted_iota(jnp.int32, (tq, tq), 0)
        kcol = lax.broadcasted_iota(jnp.int32, (tq, tq), 1)
        mask = (kcol // CHUNK) <= (qrow // CHUNK)
        _flash_update(q_ref[...], k_of, v_of, tq, m_scr, acc_scr, mask)
        _flash_finish(lam_ref[0], sub_ref, scale, y_ref, acc_scr)


def _attn_prompt(lam, q, kb, vb, subln, scale, batch, seq):
    n, dd = q.shape
    tq = min(512, seq)
    nq = seq // tq
    nh2 = 2 * dd // PAIR
    pairs = [(i, j) for i in range(nq) for j in range(i + 1)]
    qt = jnp.array([p[0] for p in pairs], jnp.int32)
    kt = jnp.array([p[1] for p in pairs], jnp.int32)
    kv_spec = pl.BlockSpec((tq, dd), lambda b, s, qt, kt: (b * nq + kt[s], 0))
    q_spec = pl.BlockSpec((tq, dd), lambda b, s, qt, kt: (b * nq + qt[s], 0))
    return pl.pallas_call(
        functools.partial(_attn_prompt_body, tq=tq, scale=scale),
        grid_spec=pltpu.PrefetchScalarGridSpec(
            num_scalar_prefetch=2,
            grid=(batch, len(pairs)),
            in_specs=[pl.BlockSpec(memory_space=pltpu.SMEM), q_spec, kv_spec, kv_spec,
                      pl.BlockSpec((1, PAIR), lambda b, s, qt, kt: (0, 0))],
            out_specs=q_spec,
            scratch_shapes=[pltpu.VMEM((nh2, tq, PAIR), F32), pltpu.VMEM((nh2, tq, 2 * PAIR), F32)]),
        out_shape=jax.ShapeDtypeStruct((n, dd), BF16),
        compiler_params=_params("arbitrary", "arbitrary"),
        name="attn_prompt",
    )(qt, kt, lam, q, kb, vb, subln)


def _attn_sample_body(lam_ref, q_ref, ck_ref, cv_ref, k_ref, v_ref, sub_ref, y_ref, m_scr, acc_scr, *, scale):
    kj = pl.program_id(1)
    last = pl.num_programs(1) - 1

    @pl.when(kj == 0)
    def _():
        _flash_init(m_scr, acc_scr)

    head = lambda h: slice(h * PAIR, (h + 1) * PAIR)

    nh = q_ref.shape[1] // PAIR
    tk = ck_ref.shape[0] // nh

    @pl.when(kj < last)
    def _():
        _flash_update(q_ref[...], lambda h: ck_ref[pl.ds(h, tk, stride=nh), :].astype(BF16),
                      lambda h: cv_ref[pl.ds(h, tk, stride=nh), :].astype(BF16), tk, m_scr, acc_scr)

    @pl.when(kj == last)
    def _():
        _flash_update(q_ref[...], lambda h: k_ref[:, head(h)], lambda h: v_ref[:, head(h)],
                      k_ref.shape[0], m_scr, acc_scr)
        _flash_finish(lam_ref[0], sub_ref, scale, y_ref, acc_scr)


def _attn_sample(lam, q, cache_k, cache_v, layer, kb, vb, subln, scale, batch, seq):
    n, dd = q.shape
    past = cache_k.shape[2]
    tk = min(1024, past)
    nk = past // tk
    nh2 = 2 * dd // PAIR
    row = pl.BlockSpec((seq, dd), lambda b, j: (b, 0))
    nh = dd // PAIR
    cache_k = cache_k.reshape(cache_k.shape[:2] + (past * nh, PAIR))
    cache_v = cache_v.reshape(cache_v.shape[:2] + (past * nh, PAIR))
    cache = pl.BlockSpec((None, None, tk * nh, PAIR), lambda b, j: (layer, b, jnp.minimum(j, nk - 1), 0))
    return pl.pallas_call(
        functools.partial(_attn_sample_body, scale=scale),
        grid=(batch, nk + 1),
        in_specs=[pl.BlockSpec(memory_space=pltpu.SMEM), row, cache, cache, row, row, _const_spec((1, PAIR), 2)],
        out_specs=row,
        out_shape=jax.ShapeDtypeStruct((n, dd), BF16),
        scratch_shapes=[pltpu.VMEM((nh2, seq, PAIR), F32), pltpu.VMEM((nh2, seq, 2 * PAIR), F32)],
        compiler_params=_params("arbitrary", "arbitrary"),
        name="attn_sample",
    )(lam, q, cache_k, cache_v, kb, vb, subln)


def _ffn_body(h_ref, yr_ref, ya_ref, wo_ref, gf_ref, wup_ref, cw_ref, cb_ref, wdn_ref, cprev_ref,
              hout_ref, cnew_ref, carry_scr, *, dff, fc):
    t = pl.program_id(1)

    @pl.when(t == 0)
    def _():
        carry_scr[...] = cprev_ref[...]

    y = jnp.concatenate([yr_ref[...], ya_ref[...]], axis=1)
    h1 = h_ref[...] + _dg(y, wo_ref[...], NN)
    ms = jnp.mean(h1 * h1, axis=-1, keepdims=True)
    xn = (h1 * lax.rsqrt(ms + NORM_EPS) * gf_ref[...]).astype(BF16)
    tm = h1.shape[0]
    rowi = lax.broadcasted_iota(jnp.int32, (tm, fc), 0)
    acc = jnp.zeros(h1.shape, F32)

    def up(c):
        return (_dg(xn, wup_ref[:, c * fc:(c + 1) * fc], NN),
                _dg(xn, wup_ref[:, dff + c * fc:dff + (c + 1) * fc], NN))

    nxt = up(0)
    for c in range(dff // fc):
        sl = slice(c * fc, (c + 1) * fc)
        gt, u = nxt
        if c + 1 < dff // fc:
            nxt = up(c + 1)
        p1 = carry_scr[7:8, sl]
        p2 = carry_scr[6:7, sl]
        g1 = jnp.where(rowi == 0, p1, pltpu.roll(gt, 1, axis=0))
        g2 = jnp.where(rowi == 0, p2, jnp.where(rowi == 1, p1, pltpu.roll(gt, 2, axis=0)))
        gc = cb_ref[:, sl] + g2 * cw_ref[0:1, sl] + g1 * cw_ref[1:2, sl] + gt * cw_ref[2:3, sl]
        carry_scr[:, sl] = gt[tm - 8:, :]
        hh = (gc * jax.nn.sigmoid(gc) * u).astype(BF16)
        acc = acc + _dg(hh, wdn_ref[sl, :], NN)
    hout_ref[...] = h1 + acc
    cnew_ref[...] = carry_scr[...]


def _ffn(h, yr, ya, wo, g_ffn, wup, cw, cb, wdn, conv_prev8, batch, seq):
    n, d = h.shape
    dff = cb.shape[-1]
    fc = 256
    tm = min(512, seq)
    nt = seq // tm
    row = lambda w: pl.BlockSpec((tm, w), lambda b, t: (b * nt + t, 0))
    st = pl.BlockSpec((None, 8, dff), lambda b, t: (b, 0, 0))
    cs = lambda a: _const_spec(a.shape, 2)
    return pl.pallas_call(
        functools.partial(_ffn_body, dff=dff, fc=fc),
        grid=(batch, nt),
        in_specs=[row(d), row(yr.shape[1]), row(ya.shape[1]), cs(wo), cs(g_ffn), cs(wup), cs(cw), cs(cb), cs(wdn), st],
        out_specs=[row(d), st],
        out_shape=[jax.ShapeDtypeStruct((n, d), F32), jax.ShapeDtypeStruct((batch, 8, dff), F32)],
        scratch_shapes=[pltpu.VMEM((8, dff), F32)],
        compiler_params=_params("arbitrary", "arbitrary"),
        name="ffn",
    )(h, yr, ya, wo, g_ffn, wup, cw, cb, wdn, conv_prev8)


def _stack_heads_body(*refs, depth):
    layer = pl.program_id(0)
    o_ref = refs[depth]
    for i in range(depth):
        @pl.when(layer == i)
        def _(i=i):
            for h in range(o_ref.shape[1]):
                o_ref[:, h, :] = refs[i][:, h * PAIR:(h + 1) * PAIR]


def _stack_heads(xs):
    depth = len(xs)
    n, dd = xs[0].shape
    nh = dd // PAIR
    tm = min(1024, n)

    def in_spec(i):
        return pl.BlockSpec((tm, dd), lambda l, t: (jnp.where(l == i, t, 0), 0))

    return pl.pallas_call(
        functools.partial(_stack_heads_body, depth=depth),
        grid=(depth, n // tm),
        in_specs=[in_spec(i) for i in range(depth)],
        out_specs=pl.BlockSpec((None, tm, nh, PAIR), lambda l, t: (l, t, 0, 0)),
        out_shape=jax.ShapeDtypeStruct((depth, n, nh, PAIR), xs[0].dtype),
        compiler_params=_params("arbitrary", "arbitrary"),
        name="stack_heads",
    )(*xs)


PREP_ROWS = 512
PREP_SUB = 256
SCAN_ROWS = 512


def _lambda_init(layer):
    return 0.8 - 0.6 * math.exp(-0.3 * layer)


def _run_group(x, depth, layers, ones_pair, shift_prev, wkv_prev, conv_prev, attend):
    batch, seq, d = x.shape
    h = x.reshape(batch * seq, d)
    ks, vs, ws, ss, cs = [], [], [], [], []
    for l in range(depth):
        lp = layers[l]
        ns = lp["mu"].shape[-1]
        dd = lp["q_gain"].shape[-1]
        dr = lp["w0"].shape[-1]
        dff = lp["conv_b"].shape[-1]
        ps, q, k, v, kb, vb = _inproj(h, lp["g_mix"], lp["w_in"], lp["q_gain"], lp["k_gain"], ns, dd)

        sp = jnp.zeros((batch, 1, ns), F32) if shift_prev is None else shift_prev[l][:, None, :]
        if wkv_prev is None:
            h0 = jnp.zeros((batch, HEAD, dr), F32)
        else:
            h0 = jnp.transpose(wkv_prev[l], (0, 3, 1, 2)).reshape(batch, HEAD, dr)
        rp, y0, pc, hinc, bonus, g = _wkv_prep(ps, sp, lp, ones_pair, batch, seq)
        yr, hout = _wkv_scan(rp, y0, pc, hinc, bonus, g, h0, lp["gn_w"], lp["gn_b"], ones_pair, batch, seq)

        lam_init = _lambda_init(l)
        ya = attend(l, lp["lam"], q, kb, vb, lp["subln"], 1.0 - lam_init, batch, seq)

        cp = jnp.zeros((batch, 8, dff), F32) if conv_prev is None else jnp.pad(conv_prev[l], ((0, 0), (6, 0), (0, 0)))
        h, cnew = _ffn(h, yr, ya, lp["w_out"], lp["g_ffn"], lp["w_up"], lp["conv_w"], lp["conv_b"], lp["w_down"],
                       cp, batch, seq)

        ks.append(k)
        vs.append(v)
        ws.append(jnp.transpose(hout.reshape(batch, HEAD, dr // HEAD, HEAD), (0, 2, 3, 1)))
        ss.append(ps.reshape(batch, seq, ns)[:, -1])
        cs.append(cnew[:, 6:8])
    heads = lambda xs: _stack_heads(xs).reshape(depth, batch, seq, -1, PAIR)
    return (h.reshape(batch, seq, d), heads(ks), heads(vs), jnp.stack(ws), jnp.stack(ss), jnp.stack(cs))


def kernel(x_prompt, x_sample, cache_k, cache_v, state_wkv, state_shift, state_conv, g_mix, w_in, mu_shift, w0, w_decay, a0, w_aaa, w_gate, k_k, k_a, r_k, gn_w, gn_b, q_gain, k_gain, lambdas, subln_gain, w_out, g_ffn, w_ffn_in, conv_w, conv_b, w_ffn_out):
    depth = w_in.shape[0]
    dr = w0.shape[-1]
    ns = mu_shift.shape[-1]
    dd = (w_in.shape[-1] - ns) // 3
    assert dr % PAIR == 0 and dd % PAIR == 0 and q_gain.shape[-1] == HEAD and subln_gain.shape[-1] == PAIR
    assert x_prompt.shape[1] % CHUNK == 0 and x_sample.shape[1] == CHUNK

    row = lambda a: a.reshape(1, -1).astype(F32)
    layers = []
    for l in range(depth):
        lv = lambdas[l].astype(F32)
        lam = jnp.exp(jnp.sum(lv[0] * lv[1])) - jnp.exp(jnp.sum(lv[2] * lv[3])) + _lambda_init(l)
        layers.append(dict(
            g_mix=row(g_mix[l]), w_in=w_in[l].astype(BF16), mu=row(mu_shift[l]), w0=row(w0[l]), a0=row(a0[l]),
            w_decay=w_decay[l], w_aaa=w_aaa[l], w_gate=w_gate[l], k_k=row(k_k[l]), k_a=row(k_a[l]), r_k=row(r_k[l]),
            gn_w=row(gn_w[l]), gn_b=row(gn_b[l]),
            q_gain=row(jnp.tile(q_gain[l], dd // HEAD)), k_gain=row(jnp.tile(k_gain[l], dd // HEAD)),
            lam=lam.reshape(1), subln=row(subln_gain[l]), w_out=w_out[l].astype(BF16), g_ffn=row(g_ffn[l]),
            w_up=w_ffn_in[l].astype(BF16), conv_w=conv_w[l], conv_b=row(conv_b[l]), w_down=w_ffn_out[l].astype(BF16)))

    ones_pair = jnp.kron(jnp.eye(2, dtype=F32), jnp.ones((HEAD, HEAD), F32)).astype(BF16)

    def attend_prompt(l, lam, q, kb, vb, subln, scale, batch, seq):
        return _attn_prompt(lam, q, kb, vb, subln, scale, batch, seq)


    def attend_sample(l, lam, q, kb, vb, subln, scale, batch, seq):
        return _attn_sample(lam, q, cache_k, cache_v, l, kb, vb, subln, scale, batch, seq)

    yp, pk, pv, pw, ps_, pc = _run_group(x_prompt, depth, layers, ones_pair, None, None, None, attend_prompt)
    ys, sk, sv, sw, ss, sc = _run_group(x_sample, depth, layers, ones_pair, state_shift, state_wkv, state_conv,
                                        attend_sample)
    return (yp, ys, pk, pv, pw, ps_, pc, sk, sv, sw, ss, sc)
```

```python
import functools
import math

import jax
import jax.numpy as jnp
from jax import lax
from jax.experimental import pallas as pl
from jax.experimental.pallas import tpu as pltpu

F32 = jnp.float32
BF16 = jnp.bfloat16

HEAD = 64
PAIR = 2 * HEAD
CHUNK = 64
NORM_EPS = 1e-6
GN_EPS = 64e-5
NEG_BIG = -1e30
LOG2E = math.log2(math.e)
VMEM_LIMIT = 56 * 1024 * 1024

NN = (((1,), (0,)), ((), ()))
NT = (((1,), (1,)), ((), ()))
TN = (((0,), (0,)), ((), ()))


def _dg(a, b, dn):
    return lax.dot_general(a, b, dn, preferred_element_type=F32)


def _split2(x):
    hi = x.astype(BF16)
    lo = (x - hi.astype(F32)).astype(BF16)
    return hi, lo


def _mm(a, b, dn=NN):
    return _dg(a.astype(BF16), b.astype(BF16), dn)


def _mm3(a, b):
    ah, al = _split2(a)
    bh, bl = _split2(b)
    return _dg(ah, bh, NN) + (_dg(ah, bl, NN) + _dg(al, bh, NN))


def _mm_exact_rhs(a, e):
    hi, lo = _split2(a)
    return _dg(hi, e, NN) + _dg(lo, e, NN)


def _mm_exact_lhs(e, b):
    hi, lo = _split2(b)
    return _dg(e, hi, NN) + _dg(e, lo, NN)


def _segsum(x, ones_bd):
    w = ones_bd.shape[0]
    parts = [_mm_exact_rhs(x[:, i:i + w], ones_bd) for i in range(0, x.shape[1], w)]
    return parts[0] if len(parts) == 1 else jnp.concatenate(parts, axis=1)


def _params(*sem):
    return pltpu.CompilerParams(dimension_semantics=sem, vmem_limit_bytes=VMEM_LIMIT)


def _const_spec(shape, grid_rank):
    zeros = (0,) * len(shape)
    if grid_rank == 1:
        return pl.BlockSpec(shape, lambda i: zeros, pipeline_mode=pl.Buffered(1))
    if grid_rank == 2:
        return pl.BlockSpec(shape, lambda i, j: zeros, pipeline_mode=pl.Buffered(1))
    return pl.BlockSpec(shape, lambda i, j, k: zeros, pipeline_mode=pl.Buffered(1))


def _inproj_body(x_ref, g_ref, w_ref, qg_ref, kg_ref, ones_ref, kall_ref, vall_ref,
                 ps_ref, q_ref, k_ref, v_ref, kb_ref, vb_ref, *, ns, dd):
    del kall_ref, vall_ref
    x = x_ref[...]
    ms = jnp.mean(x * x, axis=-1, keepdims=True)
    xn = (x * lax.rsqrt(ms + NORM_EPS) * g_ref[...]).astype(BF16)
    ps_ref[...] = _dg(xn, w_ref[:, :ns], NN)
    ones = ones_ref[...]

    def head_norm(t, gain):
        w = ones.shape[0]
        sq = (t * t).astype(BF16)
        ss = jnp.concatenate([_dg(sq[:, i:i + w], ones, NN) for i in range(0, dd, w)], axis=1)
        return t * lax.rsqrt(ss * (1.0 / HEAD) + NORM_EPS) * gain

    q = _dg(xn, w_ref[:, ns:ns + dd], NN)
    k = _dg(xn, w_ref[:, ns + dd:ns + 2 * dd], NN)
    v = _dg(xn, w_ref[:, ns + 2 * dd:ns + 3 * dd], NN)
    nh = dd // PAIR
    tm = x.shape[0]

    def put_heads(ref, t):
        for hd in range(nh):
            ref[pl.ds(hd, tm, stride=nh), :] = t[:, hd * PAIR:(hd + 1) * PAIR]

    put_heads(v_ref, v)
    vb_ref[...] = v.astype(BF16)
    q = head_norm(q, qg_ref[...])
    q_ref[...] = (q * (HEAD ** -0.5 * LOG2E)).astype(BF16)
    k = head_norm(k, kg_ref[...])
    put_heads(k_ref, k)
    kb_ref[...] = k.astype(BF16)


def _inproj(h, g, w_bf, q_gain, k_gain, k_all, v_all, layer, ns, dd):
    n, d = h.shape
    tm = min(512, n)
    nh = dd // PAIR
    leaf = pl.BlockSpec((None, tm * nh, PAIR), lambda i: (layer, i, 0))
    anywhere = pl.BlockSpec(memory_space=pl.ANY)
    ones = jnp.kron(jnp.eye(2 * PAIR // HEAD, dtype=F32), jnp.ones((HEAD, HEAD), F32)).astype(BF16)
    row = lambda w: pl.BlockSpec((tm, w), lambda i: (i, 0))
    return pl.pallas_call(
        functools.partial(_inproj_body, ns=ns, dd=dd),
        grid=(n // tm,),
        in_specs=[row(d), _const_spec((1, d), 1), _const_spec(w_bf.shape, 1),
                  _const_spec((1, dd), 1), _const_spec((1, dd), 1), _const_spec(ones.shape, 1),
                  anywhere, anywhere],
        out_specs=[row(ns), row(dd), leaf, leaf, row(dd), row(dd)],
        out_shape=[jax.ShapeDtypeStruct((n, ns), F32), jax.ShapeDtypeStruct((n, dd), BF16),
                   jax.ShapeDtypeStruct(k_all.shape, F32), jax.ShapeDtypeStruct(v_all.shape, F32),
                   jax.ShapeDtypeStruct((n, dd), BF16), jax.ShapeDtypeStruct((n, dd), BF16)],
        input_output_aliases={6: 2, 7: 3},
        compiler_params=_params("arbitrary"),
        name="inproj",
    )(h, g, w_bf, q_gain, k_gain, ones, k_all, v_all)


def _bd2(x, m0, swap=False):
    zero = jnp.zeros_like(x)
    first = jnp.where(m0, x, zero)
    second = jnp.where(m0, zero, x)
    return jnp.concatenate([second, first] if swap else [first, second], axis=0)


def _wkv_body(ps_ref, prev_ref, sp_ref, mu_ref, w0_ref, a0_ref, kk_ref, ka_ref, rk_ref,
              wd_ref, wa_ref, wg_ref, ones_ref, tril_ref, h0_ref, gnw_ref, gnb_ref,
              y_ref, hout_ref, h_scr, *, dr, nd, na, sub):
    c = pl.program_id(1)

    @pl.when(c == 0)
    def _():
        h_scr[...] = h0_ref[...]

    ns = ps_ref.shape[1]
    ones = ones_ref[...]
    sigmoid = lambda t: 0.5 + 0.5 * jnp.tanh(0.5 * t)
    seg = lambda t: jnp.concatenate([_dg(t[:, i:i + PAIR].astype(BF16), ones, NN) for i in range(0, dr, PAIR)],
                                    axis=1)

    lane = lax.broadcasted_iota(jnp.int32, (CHUNK, PAIR), 1)
    rr = lax.broadcasted_iota(jnp.int32, (CHUNK, PAIR), 0)
    m0 = lane < HEAD
    lane_in = jnp.where(m0, lane, lane - HEAD)
    strict = rr > lane_in
    incl = rr >= lane_in
    diag = rr == lane_in
    eye2 = jnp.where(diag, 1.0, 0.0).astype(F32)
    m0w = lax.broadcasted_iota(jnp.int32, (2 * CHUNK, PAIR), 1) < HEAD
    zero = jnp.zeros((CHUNK, PAIR), F32)
    zero2 = jnp.zeros((2 * CHUNK, PAIR), F32)

    bd = lambda t, swap=False: _bd2(t, m0, swap).astype(BF16)

    def prologue(i, pro):
        rows = slice(i * sub, (i + 1) * sub)
        x = ps_ref[rows, :]
        if i == 0:
            row0 = jnp.where(c == 0, sp_ref[...], prev_ref[7:8, :])
        else:
            row0 = ps_ref[i * sub - 1:i * sub, :]
        rowi = lax.broadcasted_iota(jnp.int32, (sub, ns), 0)
        prev = jnp.where(rowi == 0, row0, pltpu.roll(x, 1, axis=0))
        xs = x + (prev - x) * mu_ref[...]
        r = xs[:, :dr]
        k = xs[:, dr:2 * dr]
        v = xs[:, 2 * dr:3 * dr]
        o = 3 * dr
        t_w = jnp.tanh(xs[:, o:o + nd])
        xa = xs[:, o + nd:o + nd + na]
        s_g = sigmoid(xs[:, o + nd + na:])
        yield
        z = _mm(t_w, wd_ref[...])
        al = _mm(xa, wa_ref[...])
        gate = _mm(s_g, wg_ref[...])
        yield
        ld = -math.exp(-0.5) * sigmoid(w0_ref[...] + z)
        a = sigmoid(a0_ref[...] + al)
        kk = k * kk_ref[...]
        k2 = k * (1.0 + (a - 1.0) * ka_ref[...])
        kk_ss = seg(kk * kk)
        bonus = seg(r * k2 * rk_ref[...])
        cum = _mm_exact_lhs(tril_ref[...], ld)
        yield
        kk = kk * lax.rsqrt(jnp.maximum(kk_ss, 1e-24))
        bv = kk * a
        e_cum = jnp.exp(cum)
        e_neg = jnp.exp(-cum)
        chunk_start = lax.broadcasted_iota(jnp.int32, cum.shape, 0) % CHUNK == 0
        e_prev = jnp.where(chunk_start, 1.0, pltpu.roll(e_cum, 1, axis=0))
        pro.update(at=-kk * e_prev, rt=r * e_cum, bt=bv * e_neg, kt=k2 * e_neg, bv=bv, k2=k2, v=v,
                   e_cum=e_cum, e_neg=e_neg, gate=gate, bonus=bonus * v)

    def algebra(pro, res):
        chains = [(slice(ch * CHUNK, (ch + 1) * CHUNK), slice(j * PAIR, (j + 1) * PAIR), (ch + 1) * CHUNK - 1)
                  for ch in range(sub // CHUNK) for j in range(dr // PAIR)]
        atp = [pro["at"][rs, sl] for rs, sl, _ in chains]
        rtp = [pro["rt"][rs, sl] for rs, sl, _ in chains]
        vp = [pro["v"][rs, sl] for rs, sl, _ in chains]
        g0, g1 = [], []
        for n, (rs, sl, _) in enumerate(chains):
            btp, ktp = pro["bt"][rs, sl], pro["kt"][rs, sl]
            lhs = jnp.concatenate([atp[n], rtp[n]], axis=0)
            g0.append(_mm(jnp.where(m0w, lhs, zero2), jnp.concatenate([btp, ktp], axis=0), NT))
            g1.append(_mm(jnp.where(m0w, zero2, lhs), jnp.concatenate([ktp, btp], axis=0), NT))
        yield
        qp = [jnp.where(strict, jnp.where(m0, a[:CHUNK], b[:CHUNK]), zero) for a, b in zip(g0, g1)]
        akmk = [jnp.concatenate([jnp.where(strict, jnp.where(m0, b[:CHUNK], a[:CHUNK]), zero),
                                 jnp.where(incl, jnp.where(m0, b[CHUNK:], a[CHUNK:]), zero)], axis=0)
                for a, b in zip(g0, g1)]
        mb = [jnp.where(incl, jnp.where(m0, a[CHUNK:], b[CHUNK:]), zero) for a, b in zip(g0, g1)]
        kv = [_mm(l, bd(t, True)) for l, t in zip(akmk, vp)]
        yield
        tm = [eye2 + t for t in qp]
        bq = [bd(t) for t in qp]
        for _ in range(5):
            qp = [_mm(t, b) for t, b in zip(qp, bq)]
            yield
            bq = [bd(t) for t in qp]
            tm = [t + _mm(t, b) for t, b in zip(tm, bq)]
            yield
        au = [_mm(t, jnp.concatenate([bd(a), bd(k[:CHUNK])], axis=1)) for t, a, k in zip(tm, atp, kv)]
        yield
        ry = [_mm(t, jnp.concatenate([bd(a[:, :PAIR]), bd(a[:, PAIR:])], axis=1)) for t, a in zip(mb, au)]
        ph = []
        for n, (rs, sl, last) in enumerate(chains):
            e_end = pro["e_cum"][last:last + 1, sl] * pro["e_neg"][rs, sl]
            rhs = jnp.concatenate([au[n], jnp.concatenate([zero, vp[n]], axis=1)], axis=0)
            ph.append(_mm(jnp.concatenate([pro["bv"][rs, sl] * e_end, pro["k2"][rs, sl] * e_end], axis=0), rhs, TN))
        yield
        res["rp"] = [(rtp[n] + ry[n][:, :PAIR]).astype(BF16) for n in range(len(chains))]
        res["y0"] = [kv[n][CHUNK:] + ry[n][:, PAIR:] for n in range(len(chains))]
        res["pc"] = [(jnp.where(m0, ph[n][:CHUNK, :PAIR], ph[n][CHUNK:, :PAIR])
                      + jnp.where(diag, pro["e_cum"][last:last + 1, sl], 0.0)).astype(BF16)
                     for n, (rs, sl, last) in enumerate(chains)]
        res["hinc"] = [jnp.where(m0, ph[n][:CHUNK, PAIR:], ph[n][CHUNK:, PAIR:]) for n in range(len(chains))]

    npair = dr // PAIR

    def scan(i, pro, res, hs):
        ys = []
        for ch in range(sub // CHUNK):
            for j in range(npair):
                n = ch * npair + j
                out = _mm(jnp.concatenate([res["rp"][n], res["pc"][n]], axis=0), _bd2(hs[j], m0))
                ys.append(res["y0"][n] + out[:CHUNK])
                hs[j] = res["hinc"][n] + out[CHUNK:]
            yield
        y = jnp.concatenate([jnp.concatenate(ys[ch * npair:(ch + 1) * npair], axis=1)
                             for ch in range(sub // CHUNK)], axis=0)
        mean = _segsum(y, ones) * (1.0 / HEAD)
        d = y - mean
        var = _segsum(d * d, ones) * (1.0 / HEAD)
        yn = d * lax.rsqrt(var + GN_EPS) * gnw_ref[...] + gnb_ref[...]
        y_ref[i * sub:(i + 1) * sub, :] = ((yn + pro["bonus"]) * pro["gate"]).astype(BF16)

    nsub = ps_ref.shape[0] // sub
    pros = [dict() for _ in range(nsub)]
    hs = [h_scr[:, j * PAIR:(j + 1) * PAIR] for j in range(npair)]
    for _ in prologue(0, pros[0]):
        pass
    behind = iter(())
    for i in range(nsub):
        ahead = prologue(i + 1, pros[i + 1]) if i + 1 < nsub else iter(())
        res = {}
        for stage, _ in enumerate(algebra(pros[i], res)):
            next(ahead if stage % 2 == 0 else behind, None)
        for _ in ahead:
            pass
        for _ in behind:
            pass
        behind = scan(i, pros[i], res, hs)
    for _ in behind:
        pass
    for j in range(npair):
        h_scr[:, j * PAIR:(j + 1) * PAIR] = hs[j]
    hout_ref[...] = h_scr[...]


def _wkv(ps, shift_prev, h0, lp, ones_pair, batch, seq):
    n, ns = ps.shape
    dr = lp["w0"].shape[-1]
    nd = lp["w_decay"].shape[0]
    na = lp["w_aaa"].shape[0]
    rows = min(WKV_ROWS, seq)
    sub = min(WKV_SUB, rows)
    nc = seq // rows
    blk = rows // 8
    tril = jnp.kron(jnp.eye(sub // CHUNK, dtype=F32), jnp.tril(jnp.ones((CHUNK, CHUNK), F32))).astype(BF16)
    row = lambda w: pl.BlockSpec((rows, w), lambda b, c: (b * nc + c, 0))
    st = pl.BlockSpec((None, HEAD, dr), lambda b, c: (b, 0, 0))
    cs = lambda a: _const_spec(a.shape, 2)
    consts = [lp["mu"], lp["w0"], lp["a0"], lp["k_k"], lp["k_a"], lp["r_k"],
              lp["w_decay"], lp["w_aaa"], lp["w_gate"], ones_pair, tril]
    return pl.pallas_call(
        functools.partial(_wkv_body, dr=dr, nd=nd, na=na, sub=sub),
        grid=(batch, nc),
        in_specs=[row(ns),
                  pl.BlockSpec((8, ns), lambda b, c: (jnp.maximum((b * nc + c) * blk - 1, 0), 0)),
                  pl.BlockSpec((None, 1, ns), lambda b, c: (b, 0, 0))] + [cs(a) for a in consts]
                 + [st, cs(lp["gn_w"]), cs(lp["gn_b"])],
        out_specs=[row(dr), st],
        out_shape=[jax.ShapeDtypeStruct((n, dr), BF16), jax.ShapeDtypeStruct((batch, HEAD, dr), F32)],
        scratch_shapes=[pltpu.VMEM((HEAD, dr), F32)],
        compiler_params=_params("arbitrary", "arbitrary"),
        name="wkv",
    )(ps, ps, shift_prev, *consts, h0, lp["gn_w"], lp["gn_b"])


def _flash_init(m_scr, acc_scr):
    m_scr[...] = jnp.full(m_scr.shape, NEG_BIG, F32)
    acc_scr[...] = jnp.zeros(acc_scr.shape, F32)


def _lane_rep(x, width):
    if width % PAIR == 0:
        return x if width == PAIR else jnp.concatenate([x] * (width // PAIR), axis=1)
    return x[:, :width]


def _flash_update(q, k_of, v_of, tk, m_scr, acc_scr, mask=None):
    tq = q.shape[0]
    first = lax.broadcasted_iota(jnp.int32, (tq, PAIR), 1) < HEAD
    zq = jnp.zeros((tq, PAIR), BF16)
    ones = jnp.ones((tk, PAIR), BF16)
    n_maps = 2 * (q.shape[1] // PAIR)

    def scores(i):
        qh = q[:, (i // 2) * PAIR:(i // 2 + 1) * PAIR]
        qm = jnp.where(first, qh, zq) if i % 2 == 0 else jnp.where(first, zq, qh)
        return _dg(qm, k_of(i // 2), NT)

    s_next = scores(0)
    for i in range(n_maps):
        s = s_next
        if i + 1 < n_maps:
            s_next = scores(i + 1)
        if mask is not None:
            s = jnp.where(mask, s, NEG_BIG)
        v_aug = jnp.concatenate([v_of(i // 2), ones], axis=1)
        m_prev = m_scr[i]
        m_new = jnp.maximum(m_prev, jnp.max(s, axis=1, keepdims=True))
        alpha = jnp.exp2(m_prev - m_new)
        p = jnp.exp2((s - _lane_rep(m_new, tk)).astype(BF16))
        acc_scr[i] = _lane_rep(alpha, 2 * PAIR) * acc_scr[i] + _dg(p, v_aug, NN)
        m_scr[i] = m_new


def _flash_finish(lam, sub_ref, scale, y_ref, acc_scr):
    for h in range(y_ref.shape[1] // PAIR):
        a1, a2 = acc_scr[2 * h], acc_scr[2 * h + 1]
        o = a1[:, :PAIR] / a1[:, PAIR:] - lam * (a2[:, :PAIR] / a2[:, PAIR:])
        ms = jnp.mean(o * o, axis=-1, keepdims=True)
        y = o * lax.rsqrt(ms + NORM_EPS) * sub_ref[...] * scale
        y_ref[:, h * PAIR:(h + 1) * PAIR] = y.astype(BF16)


def _attn_prompt_body(qt_ref, kt_ref, lam_ref, q_ref, k_ref, v_ref, sub_ref, y_ref, m_scr, acc_scr, *, tq, scale):
    s = pl.program_id(1)
    qi = qt_ref[s]
    kj = kt_ref[s]
    head = lambda h: slice(h * PAIR, (h + 1) * PAIR)
    k_of = lambda h: k_ref[:, head(h)]
    v_of = lambda h: v_ref[:, head(h)]

    @pl.when(kj == 0)
    def _():
        _flash_init(m_scr, acc_scr)

    @pl.when(kj < qi)
    def _():
        _flash_update(q_ref[...], k_of, v_of, tq, m_scr, acc_scr)

    @pl.when(kj == qi)
    def _():
        qrow = lax.broadcasted_iota(jnp.int32, (tq, tq), 0)
        kcol = lax.broadcasted_iota(jnp.int32, (tq, tq), 1)
        mask = (kcol // CHUNK) <= (qrow // CHUNK)
        _flash_update(q_ref[...], k_of, v_of, tq, m_scr, acc_scr, mask)
        _flash_finish(lam_ref[0], sub_ref, scale, y_ref, acc_scr)


def _attn_prompt(lam, q, kb, vb, subln, scale, batch, seq):
    n, dd = q.shape
    tq = min(512, seq)
    nq = seq // tq
    nh2 = 2 * dd // PAIR
    pairs = [(i, j) for i in range(nq) for j in range(i + 1)]
    qt = jnp.array([p[0] for p in pairs], jnp.int32)
    kt = jnp.array([p[1] for p in pairs], jnp.int32)
    kv_spec = pl.BlockSpec((tq, dd), lambda b, s, qt, kt: (b * nq + kt[s], 0))
    q_spec = pl.BlockSpec((tq, dd), lambda b, s, qt, kt: (b * nq + qt[s], 0))
    return pl.pallas_call(
        functools.partial(_attn_prompt_body, tq=tq, scale=scale),
        grid_spec=pltpu.PrefetchScalarGridSpec(
            num_scalar_prefetch=2,
            grid=(batch, len(pairs)),
            in_specs=[pl.BlockSpec(memory_space=pltpu.SMEM), q_spec, kv_spec, kv_spec,
                      pl.BlockSpec((1, PAIR), lambda b, s, qt, kt: (0, 0))],
            out_specs=q_spec,
            scratch_shapes=[pltpu.VMEM((nh2, tq, PAIR), F32), pltpu.VMEM((nh2, tq, 2 * PAIR), F32)]),
        out_shape=jax.ShapeDtypeStruct((n, dd), BF16),
        compiler_params=_params("arbitrary", "arbitrary"),
        name="attn_prompt",
    )(qt, kt, lam, q, kb, vb, subln)


def _attn_sample_body(lam_ref, q_ref, ck_ref, cv_ref, k_ref, v_ref, sub_ref, y_ref, m_scr, acc_scr, *, scale):
    kj = pl.program_id(1)
    last = pl.num_programs(1) - 1

    @pl.when(kj == 0)
    def _():
        _flash_init(m_scr, acc_scr)

    head = lambda h: slice(h * PAIR, (h + 1) * PAIR)

    nh = q_ref.shape[1] // PAIR
    tk = ck_ref.shape[0] // nh

    @pl.when(kj < last)
    def _():
        _flash_update(q_ref[...], lambda h: ck_ref[pl.ds(h, tk, stride=nh), :].astype(BF16),
                      lambda h: cv_ref[pl.ds(h, tk, stride=nh), :].astype(BF16), tk, m_scr, acc_scr)

    @pl.when(kj == last)
    def _():
        _flash_update(q_ref[...], lambda h: k_ref[:, head(h)], lambda h: v_ref[:, head(h)],
                      k_ref.shape[0], m_scr, acc_scr)
        _flash_finish(lam_ref[0], sub_ref, scale, y_ref, acc_scr)


def _attn_sample(lam, q, cache_k, cache_v, layer, kb, vb, subln, scale, batch, seq):
    n, dd = q.shape
    past = cache_k.shape[2]
    tk = min(1024, past)
    nk = past // tk
    nh2 = 2 * dd // PAIR
    row = pl.BlockSpec((seq, dd), lambda b, j: (b, 0))
    nh = dd // PAIR
    cache_k = cache_k.reshape(cache_k.shape[:2] + (past * nh, PAIR))
    cache_v = cache_v.reshape(cache_v.shape[:2] + (past * nh, PAIR))
    cache = pl.BlockSpec((None, None, tk * nh, PAIR), lambda b, j: (layer, b, jnp.minimum(j, nk - 1), 0))
    return pl.pallas_call(
        functools.partial(_attn_sample_body, scale=scale),
        grid=(batch, nk + 1),
        in_specs=[pl.BlockSpec(memory_space=pltpu.SMEM), row, cache, cache, row, row, _const_spec((1, PAIR), 2)],
        out_specs=row,
        out_shape=jax.ShapeDtypeStruct((n, dd), BF16),
        scratch_shapes=[pltpu.VMEM((nh2, seq, PAIR), F32), pltpu.VMEM((nh2, seq, 2 * PAIR), F32)],
        compiler_params=_params("arbitrary", "arbitrary"),
        name="attn_sample",
    )(lam, q, cache_k, cache_v, kb, vb, subln)


def _ffn_body(h_ref, yr_ref, ya_ref, wo_ref, gf_ref, wup_ref, cw_ref, cb_ref, wdn_ref, cprev_ref,
              hout_ref, cnew_ref, carry_scr, *, dff, fc):
    t = pl.program_id(1)

    @pl.when(t == 0)
    def _():
        carry_scr[...] = cprev_ref[...]

    y = jnp.concatenate([yr_ref[...], ya_ref[...]], axis=1)
    h1 = h_ref[...] + _dg(y, wo_ref[...], NN)
    ms = jnp.mean(h1 * h1, axis=-1, keepdims=True)
    xn = (h1 * lax.rsqrt(ms + NORM_EPS) * gf_ref[...]).astype(BF16)
    tm = h1.shape[0]
    rowi = lax.broadcasted_iota(jnp.int32, (tm, fc), 0)
    acc = jnp.zeros(h1.shape, F32)

    def up(c):
        return (_dg(xn, wup_ref[:, c * fc:(c + 1) * fc], NN),
                _dg(xn, wup_ref[:, dff + c * fc:dff + (c + 1) * fc], NN))

    nxt = up(0)
    for c in range(dff // fc):
        sl = slice(c * fc, (c + 1) * fc)
        gt, u = nxt
        if c + 1 < dff // fc:
            nxt = up(c + 1)
        p1 = carry_scr[7:8, sl]
        p2 = carry_scr[6:7, sl]
        g1 = jnp.where(rowi == 0, p1, pltpu.roll(gt, 1, axis=0))
        g2 = jnp.where(rowi == 0, p2, jnp.where(rowi == 1, p1, pltpu.roll(gt, 2, axis=0)))
        gc = cb_ref[:, sl] + g2 * cw_ref[0:1, sl] + g1 * cw_ref[1:2, sl] + gt * cw_ref[2:3, sl]
        carry_scr[:, sl] = gt[tm - 8:, :]
        hh = (gc * jax.nn.sigmoid(gc) * u).astype(BF16)
        acc = acc + _dg(hh, wdn_ref[sl, :], NN)
    hout_ref[...] = h1 + acc
    cnew_ref[...] = carry_scr[...]


def _ffn(h, yr, ya, wo, g_ffn, wup, cw, cb, wdn, conv_prev8, batch, seq):
    n, d = h.shape
    dff = cb.shape[-1]
    fc = 256
    tm = min(512, seq)
    nt = seq // tm
    row = lambda w: pl.BlockSpec((tm, w), lambda b, t: (b * nt + t, 0))
    st = pl.BlockSpec((None, 8, dff), lambda b, t: (b, 0, 0))
    cs = lambda a: _const_spec(a.shape, 2)
    return pl.pallas_call(
        functools.partial(_ffn_body, dff=dff, fc=fc),
        grid=(batch, nt),
        in_specs=[row(d), row(yr.shape[1]), row(ya.shape[1]), cs(wo), cs(g_ffn), cs(wup), cs(cw), cs(cb), cs(wdn), st],
        out_specs=[row(d), st],
        out_shape=[jax.ShapeDtypeStruct((n, d), F32), jax.ShapeDtypeStruct((batch, 8, dff), F32)],
        scratch_shapes=[pltpu.VMEM((8, dff), F32)],
        compiler_params=_params("arbitrary", "arbitrary"),
        name="ffn",
    )(h, yr, ya, wo, g_ffn, wup, cw, cb, wdn, conv_prev8)


WKV_ROWS = 512
WKV_SUB = 256


def _lambda_init(layer):
    return 0.8 - 0.6 * math.exp(-0.3 * layer)


def _run_group(x, depth, layers, ones_pair, shift_prev, wkv_prev, conv_prev, attend):
    batch, seq, d = x.shape
    h = x.reshape(batch * seq, d)
    ws, ss, cs = [], [], []
    nh = layers[0]["q_gain"].shape[-1] // PAIR
    k_all = jnp.zeros((depth, batch * seq * nh, PAIR), F32)
    v_all = jnp.zeros((depth, batch * seq * nh, PAIR), F32)
    for l in range(depth):
        lp = layers[l]
        ns = lp["mu"].shape[-1]
        dd = lp["q_gain"].shape[-1]
        dr = lp["w0"].shape[-1]
        dff = lp["conv_b"].shape[-1]
        ps, q, k_all, v_all, kb, vb = _inproj(h, lp["g_mix"], lp["w_in"], lp["q_gain"], lp["k_gain"],
                                              k_all, v_all, l, ns, dd)

        sp = jnp.zeros((batch, 1, ns), F32) if shift_prev is None else shift_prev[l][:, None, :]
        if wkv_prev is None:
            h0 = jnp.zeros((batch, HEAD, dr), F32)
        else:
            h0 = jnp.transpose(wkv_prev[l], (0, 3, 1, 2)).reshape(batch, HEAD, dr)
        yr, hout = _wkv(ps, sp, h0, lp, ones_pair, batch, seq)

        lam_init = _lambda_init(l)
        ya = attend(l, lp["lam"], q, kb, vb, lp["subln"], 1.0 - lam_init, batch, seq)

        cp = jnp.zeros((batch, 8, dff), F32) if conv_prev is None else jnp.pad(conv_prev[l], ((0, 0), (6, 0), (0, 0)))
        h, cnew = _ffn(h, yr, ya, lp["w_out"], lp["g_ffn"], lp["w_up"], lp["conv_w"], lp["conv_b"], lp["w_down"],
                       cp, batch, seq)

        ws.append(jnp.transpose(hout.reshape(batch, HEAD, dr // HEAD, HEAD), (0, 2, 3, 1)))
        ss.append(ps.reshape(batch, seq, ns)[:, -1])
        cs.append(cnew[:, 6:8])
    leaf = lambda t: t.reshape(depth, batch, seq, nh, PAIR)
    return (h.reshape(batch, seq, d), leaf(k_all), leaf(v_all), jnp.stack(ws), jnp.stack(ss), jnp.stack(cs))


def kernel(x_prompt, x_sample, cache_k, cache_v, state_wkv, state_shift, state_conv, g_mix, w_in, mu_shift, w0, w_decay, a0, w_aaa, w_gate, k_k, k_a, r_k, gn_w, gn_b, q_gain, k_gain, lambdas, subln_gain, w_out, g_ffn, w_ffn_in, conv_w, conv_b, w_ffn_out):
    depth = w_in.shape[0]
    dr = w0.shape[-1]
    ns = mu_shift.shape[-1]
    dd = (w_in.shape[-1] - ns) // 3
    assert dr % PAIR == 0 and dd % PAIR == 0 and q_gain.shape[-1] == HEAD and subln_gain.shape[-1] == PAIR
    assert x_prompt.shape[1] % CHUNK == 0 and x_sample.shape[1] == CHUNK

    row = lambda a: a.reshape(1, -1).astype(F32)
    layers = []
    for l in range(depth):
        lv = lambdas[l].astype(F32)
        lam = jnp.exp(jnp.sum(lv[0] * lv[1])) - jnp.exp(jnp.sum(lv[2] * lv[3])) + _lambda_init(l)
        layers.append(dict(
            g_mix=row(g_mix[l]), w_in=w_in[l].astype(BF16), mu=row(mu_shift[l]), w0=row(w0[l]), a0=row(a0[l]),
            w_decay=w_decay[l], w_aaa=w_aaa[l], w_gate=w_gate[l], k_k=row(k_k[l]), k_a=row(k_a[l]), r_k=row(r_k[l]),
            gn_w=row(gn_w[l]), gn_b=row(gn_b[l]),
            q_gain=row(jnp.tile(q_gain[l], dd // HEAD)), k_gain=row(jnp.tile(k_gain[l], dd // HEAD)),
            lam=lam.reshape(1), subln=row(subln_gain[l]), w_out=w_out[l].astype(BF16), g_ffn=row(g_ffn[l]),
            w_up=w_ffn_in[l].astype(BF16), conv_w=conv_w[l], conv_b=row(conv_b[l]), w_down=w_ffn_out[l].astype(BF16)))

    ones_pair = jnp.kron(jnp.eye(2, dtype=F32), jnp.ones((HEAD, HEAD), F32)).astype(BF16)

    def attend_prompt(l, lam, q, kb, vb, subln, scale, batch, seq):
        return _attn_prompt(lam, q, kb, vb, subln, scale, batch, seq)


    def attend_sample(l, lam, q, kb, vb, subln, scale, batch, seq):
        return _attn_sample(lam, q, cache_k, cache_v, l, kb, vb, subln, scale, batch, seq)

    yp, pk, pv, pw, ps_, pc = _run_group(x_prompt, depth, layers, ones_pair, None, None, None, attend_prompt)
    ys, sk, sv, sw, ss, sc = _run_group(x_sample, depth, layers, ones_pair, state_shift, state_wkv, state_conv,
                                        attend_sample)
    return (yp, ys, pk, pv, pw, ps_, pc, sk, sv, sw, ss, sc)
```

```python
import functools
import math

import jax
import jax.numpy as jnp
from jax import lax
from jax.experimental import pallas as pl
from jax.experimental.pallas import tpu as pltpu

F32 = jnp.float32
BF16 = jnp.bfloat16

HEAD = 64
PAIR = 2 * HEAD
CHUNK = 64
NORM_EPS = 1e-6
GN_EPS = 64e-5
NEG_BIG = -1e30
LOG2E = math.log2(math.e)
VMEM_LIMIT = 56 * 1024 * 1024

NN = (((1,), (0,)), ((), ()))
NT = (((1,), (1,)), ((), ()))
TN = (((0,), (0,)), ((), ()))


def _dg(a, b, dn):
    return lax.dot_general(a, b, dn, preferred_element_type=F32)


def _split2(x):
    hi = x.astype(BF16)
    lo = (x - hi.astype(F32)).astype(BF16)
    return hi, lo


def _mm(a, b, dn=NN):
    return _dg(a.astype(BF16), b.astype(BF16), dn)


def _mm3(a, b):
    ah, al = _split2(a)
    bh, bl = _split2(b)
    return _dg(ah, bh, NN) + (_dg(ah, bl, NN) + _dg(al, bh, NN))


def _mm_exact_rhs(a, e):
    hi, lo = _split2(a)
    return _dg(hi, e, NN) + _dg(lo, e, NN)


def _mm_exact_lhs(e, b):
    hi, lo = _split2(b)
    return _dg(e, hi, NN) + _dg(e, lo, NN)


def _segsum(x, ones_bd):
    w = ones_bd.shape[0]
    parts = [_mm_exact_rhs(x[:, i:i + w], ones_bd) for i in range(0, x.shape[1], w)]
    return parts[0] if len(parts) == 1 else jnp.concatenate(parts, axis=1)


def _params(*sem):
    return pltpu.CompilerParams(dimension_semantics=sem, vmem_limit_bytes=VMEM_LIMIT)


def _const_spec(shape, grid_rank):
    zeros = (0,) * len(shape)
    if grid_rank == 1:
        return pl.BlockSpec(shape, lambda i: zeros, pipeline_mode=pl.Buffered(1))
    if grid_rank == 2:
        return pl.BlockSpec(shape, lambda i, j: zeros, pipeline_mode=pl.Buffered(1))
    return pl.BlockSpec(shape, lambda i, j, k: zeros, pipeline_mode=pl.Buffered(1))


def _inproj_body(x_ref, g_ref, w_ref, qg_ref, kg_ref, ones_ref, kall_ref, vall_ref,
                 ps_ref, q_ref, k_ref, v_ref, kb_ref, vb_ref, *, ns, dd):
    del kall_ref, vall_ref
    x = x_ref[...]
    ms = jnp.mean(x * x, axis=-1, keepdims=True)
    xn = (x * lax.rsqrt(ms + NORM_EPS) * g_ref[...]).astype(BF16)
    ps_ref[...] = _dg(xn, w_ref[:, :ns], NN)
    ones = ones_ref[...]

    def head_norm(t, gain):
        w = ones.shape[0]
        sq = (t * t).astype(BF16)
        ss = jnp.concatenate([_dg(sq[:, i:i + w], ones, NN) for i in range(0, dd, w)], axis=1)
        return t * lax.rsqrt(ss * (1.0 / HEAD) + NORM_EPS) * gain

    q = _dg(xn, w_ref[:, ns:ns + dd], NN)
    k = _dg(xn, w_ref[:, ns + dd:ns + 2 * dd], NN)
    v = _dg(xn, w_ref[:, ns + 2 * dd:ns + 3 * dd], NN)
    nh = dd // PAIR
    tm = x.shape[0]

    def put_heads(ref, t):
        for hd in range(nh):
            ref[pl.ds(hd, tm, stride=nh), :] = t[:, hd * PAIR:(hd + 1) * PAIR]

    put_heads(v_ref, v)
    vb_ref[...] = v.astype(BF16)
    q = head_norm(q, qg_ref[...])
    q_ref[...] = (q * (HEAD ** -0.5 * LOG2E)).astype(BF16)
    k = head_norm(k, kg_ref[...])
    put_heads(k_ref, k)
    kb_ref[...] = k.astype(BF16)


def _inproj(h, g, w_bf, q_gain, k_gain, k_all, v_all, layer, ns, dd):
    n, d = h.shape
    tm = min(512, n)
    nh = dd // PAIR
    leaf = pl.BlockSpec((None, tm * nh, PAIR), lambda i: (layer, i, 0))
    anywhere = pl.BlockSpec(memory_space=pl.ANY)
    ones = jnp.kron(jnp.eye(2 * PAIR // HEAD, dtype=F32), jnp.ones((HEAD, HEAD), F32)).astype(BF16)
    row = lambda w: pl.BlockSpec((tm, w), lambda i: (i, 0))
    return pl.pallas_call(
        functools.partial(_inproj_body, ns=ns, dd=dd),
        grid=(n // tm,),
        in_specs=[row(d), _const_spec((1, d), 1), _const_spec(w_bf.shape, 1),
                  _const_spec((1, dd), 1), _const_spec((1, dd), 1), _const_spec(ones.shape, 1),
                  anywhere, anywhere],
        out_specs=[row(ns), row(dd), leaf, leaf, row(dd), row(dd)],
        out_shape=[jax.ShapeDtypeStruct((n, ns), F32), jax.ShapeDtypeStruct((n, dd), BF16),
                   jax.ShapeDtypeStruct(k_all.shape, F32), jax.ShapeDtypeStruct(v_all.shape, F32),
                   jax.ShapeDtypeStruct((n, dd), BF16), jax.ShapeDtypeStruct((n, dd), BF16)],
        input_output_aliases={6: 2, 7: 3},
        compiler_params=_params("arbitrary"),
        name="inproj",
    )(h, g, w_bf, q_gain, k_gain, ones, k_all, v_all)


def _bd2(x, m0, swap=False):
    zero = jnp.zeros_like(x)
    first = jnp.where(m0, x, zero)
    second = jnp.where(m0, zero, x)
    return jnp.concatenate([second, first] if swap else [first, second], axis=0)


def _wkv_body(ps_ref, prev_ref, sp_ref, mu_ref, w0_ref, a0_ref, kk_ref, ka_ref, rk_ref,
              wd_ref, wa_ref, wg_ref, ones_ref, tril_ref, h0_ref, gnw_ref, gnb_ref,
              y_ref, hout_ref, h_scr, *, dr, nd, na, sub):
    c = pl.program_id(1)

    @pl.when(c == 0)
    def _():
        h_scr[...] = h0_ref[...]

    ns = ps_ref.shape[1]
    ones = ones_ref[...]
    sigmoid = lambda t: 0.5 + 0.5 * jnp.tanh(0.5 * t)
    seg = lambda t: jnp.concatenate([_dg(t[:, i:i + PAIR].astype(BF16), ones, NN) for i in range(0, dr, PAIR)],
                                    axis=1)

    lane = lax.broadcasted_iota(jnp.int32, (CHUNK, PAIR), 1)
    rr = lax.broadcasted_iota(jnp.int32, (CHUNK, PAIR), 0)
    m0 = lane < HEAD
    lane_in = jnp.where(m0, lane, lane - HEAD)
    strict = rr > lane_in
    incl = rr >= lane_in
    diag = rr == lane_in
    eye2 = jnp.where(diag, 1.0, 0.0).astype(F32)
    m0w = lax.broadcasted_iota(jnp.int32, (2 * CHUNK, PAIR), 1) < HEAD
    zero = jnp.zeros((CHUNK, PAIR), F32)
    zero2 = jnp.zeros((2 * CHUNK, PAIR), F32)

    bd = lambda t, swap=False: _bd2(t, m0, swap).astype(BF16)

    def prologue(i, pro):
        rows = slice(i * sub, (i + 1) * sub)
        x = ps_ref[rows, :]
        if i == 0:
            row0 = jnp.where(c == 0, sp_ref[...], prev_ref[7:8, :])
        else:
            row0 = ps_ref[i * sub - 1:i * sub, :]
        rowi = lax.broadcasted_iota(jnp.int32, (sub, ns), 0)
        prev = jnp.where(rowi == 0, row0, pltpu.roll(x, 1, axis=0))
        xs = x + (prev - x) * mu_ref[...]
        r = xs[:, :dr]
        k = xs[:, dr:2 * dr]
        v = xs[:, 2 * dr:3 * dr]
        o = 3 * dr
        t_w = jnp.tanh(xs[:, o:o + nd])
        xa = xs[:, o + nd:o + nd + na]
        s_g = sigmoid(xs[:, o + nd + na:])
        yield
        z = _mm(t_w, wd_ref[...])
        al = _mm(xa, wa_ref[...])
        gate = _mm(s_g, wg_ref[...])
        yield
        ld = -math.exp(-0.5) * sigmoid(w0_ref[...] + z)
        a = sigmoid(a0_ref[...] + al)
        kk = k * kk_ref[...]
        k2 = k * (1.0 + (a - 1.0) * ka_ref[...])
        kk_ss = seg(kk * kk)
        bonus = seg(r * k2 * rk_ref[...])
        cum = _mm_exact_lhs(tril_ref[...], ld)
        yield
        kk = kk * lax.rsqrt(jnp.maximum(kk_ss, 1e-24))
        bv = kk * a
        e_cum = jnp.exp(cum)
        e_neg = jnp.exp(-cum)
        chunk_start = lax.broadcasted_iota(jnp.int32, cum.shape, 0) % CHUNK == 0
        e_prev = jnp.where(chunk_start, 1.0, pltpu.roll(e_cum, 1, axis=0))
        pro.update(at=-kk * e_prev, rt=r * e_cum, bt=bv * e_neg, kt=k2 * e_neg, bv=bv, k2=k2, v=v,
                   e_cum=e_cum, e_neg=e_neg, gate=gate, bonus=bonus * v)

    def algebra(pro, res):
        chains = [(slice(ch * CHUNK, (ch + 1) * CHUNK), slice(j * PAIR, (j + 1) * PAIR), (ch + 1) * CHUNK - 1)
                  for ch in range(sub // CHUNK) for j in range(dr // PAIR)]
        atp = [pro["at"][rs, sl] for rs, sl, _ in chains]
        rtp = [pro["rt"][rs, sl] for rs, sl, _ in chains]
        vp = [pro["v"][rs, sl] for rs, sl, _ in chains]
        g0, g1 = [], []
        for n, (rs, sl, _) in enumerate(chains):
            btp, ktp = pro["bt"][rs, sl], pro["kt"][rs, sl]
            lhs = jnp.concatenate([atp[n], rtp[n]], axis=0)
            g0.append(_mm(jnp.where(m0w, lhs, zero2), jnp.concatenate([btp, ktp], axis=0), NT))
            g1.append(_mm(jnp.where(m0w, zero2, lhs), jnp.concatenate([ktp, btp], axis=0), NT))
        yield
        qp = [jnp.where(strict, jnp.where(m0, a[:CHUNK], b[:CHUNK]), zero) for a, b in zip(g0, g1)]
        akmk = [jnp.concatenate([jnp.where(strict, jnp.where(m0, b[:CHUNK], a[:CHUNK]), zero),
                                 jnp.where(incl, jnp.where(m0, b[CHUNK:], a[CHUNK:]), zero)], axis=0)
                for a, b in zip(g0, g1)]
        mb = [jnp.where(incl, jnp.where(m0, a[CHUNK:], b[CHUNK:]), zero) for a, b in zip(g0, g1)]
        kv = [_mm(l, bd(t, True)) for l, t in zip(akmk, vp)]
        yield
        tm = [eye2 + t for t in qp]
        bq = [bd(t) for t in qp]
        for _ in range(5):
            qp = [_mm(t, b) for t, b in zip(qp, bq)]
            yield
            bq = [bd(t) for t in qp]
            tm = [t + _mm(t, b) for t, b in zip(tm, bq)]
            yield
        au = [_mm(t, jnp.concatenate([bd(a), bd(k[:CHUNK])], axis=1)) for t, a, k in zip(tm, atp, kv)]
        yield
        ry = [_mm(t, jnp.concatenate([bd(a[:, :PAIR]), bd(a[:, PAIR:])], axis=1)) for t, a in zip(mb, au)]
        ph = []
        for n, (rs, sl, last) in enumerate(chains):
            e_end = pro["e_cum"][last:last + 1, sl] * pro["e_neg"][rs, sl]
            rhs = jnp.concatenate([au[n], jnp.concatenate([zero, vp[n]], axis=1)], axis=0)
            ph.append(_mm(jnp.concatenate([pro["bv"][rs, sl] * e_end, pro["k2"][rs, sl] * e_end], axis=0), rhs, TN))
        yield
        res["rp"] = [(rtp[n] + ry[n][:, :PAIR]).astype(BF16) for n in range(len(chains))]
        res["y0"] = [kv[n][CHUNK:] + ry[n][:, PAIR:] for n in range(len(chains))]
        res["pc"] = [(jnp.where(m0, ph[n][:CHUNK, :PAIR], ph[n][CHUNK:, :PAIR])
                      + jnp.where(diag, pro["e_cum"][last:last + 1, sl], 0.0)).astype(BF16)
                     for n, (rs, sl, last) in enumerate(chains)]
        res["hinc"] = [jnp.where(m0, ph[n][:CHUNK, PAIR:], ph[n][CHUNK:, PAIR:]) for n in range(len(chains))]

    npair = dr // PAIR

    def scan(i, pro, res, hs):
        ys = []
        for ch in range(sub // CHUNK):
            for j in range(npair):
                n = ch * npair + j
                out = _mm(jnp.concatenate([res["rp"][n], res["pc"][n]], axis=0), _bd2(hs[j], m0))
                ys.append(res["y0"][n] + out[:CHUNK])
                hs[j] = res["hinc"][n] + out[CHUNK:]
            yield
        y = jnp.concatenate([jnp.concatenate(ys[ch * npair:(ch + 1) * npair], axis=1)
                             for ch in range(sub // CHUNK)], axis=0)
        mean = _segsum(y, ones) * (1.0 / HEAD)
        d = y - mean
        var = _segsum(d * d, ones) * (1.0 / HEAD)
        yn = d * lax.rsqrt(var + GN_EPS) * gnw_ref[...] + gnb_ref[...]
        y_ref[i * sub:(i + 1) * sub, :] = ((yn + pro["bonus"]) * pro["gate"]).astype(BF16)

    nsub = ps_ref.shape[0] // sub
    pros = [dict() for _ in range(nsub)]
    hs = [h_scr[:, j * PAIR:(j + 1) * PAIR] for j in range(npair)]
    for _ in prologue(0, pros[0]):
        pass
    behind = iter(())
    for i in range(nsub):
        ahead = prologue(i + 1, pros[i + 1]) if i + 1 < nsub else iter(())
        res = {}
        for stage, _ in enumerate(algebra(pros[i], res)):
            next(ahead if stage % 2 == 0 else behind, None)
        for _ in ahead:
            pass
        for _ in behind:
            pass
        behind = scan(i, pros[i], res, hs)
    for _ in behind:
        pass
    for j in range(npair):
        h_scr[:, j * PAIR:(j + 1) * PAIR] = hs[j]
    hout_ref[...] = h_scr[...]


def _wkv(ps, shift_prev, h0, lp, ones_pair, batch, seq):
    n, ns = ps.shape
    dr = lp["w0"].shape[-1]
    nd = lp["w_decay"].shape[0]
    na = lp["w_aaa"].shape[0]
    rows = min(WKV_ROWS, seq)
    sub = min(WKV_SUB, rows)
    nc = seq // rows
    blk = rows // 8
    tril = jnp.kron(jnp.eye(sub // CHUNK, dtype=F32), jnp.tril(jnp.ones((CHUNK, CHUNK), F32))).astype(BF16)
    row = lambda w: pl.BlockSpec((rows, w), lambda b, c: (b * nc + c, 0))
    st = pl.BlockSpec((None, HEAD, dr), lambda b, c: (b, 0, 0))
    cs = lambda a: _const_spec(a.shape, 2)
    consts = [lp["mu"], lp["w0"], lp["a0"], lp["k_k"], lp["k_a"], lp["r_k"],
              lp["w_decay"], lp["w_aaa"], lp["w_gate"], ones_pair, tril]
    return pl.pallas_call(
        functools.partial(_wkv_body, dr=dr, nd=nd, na=na, sub=sub),
        grid=(batch, nc),
        in_specs=[row(ns),
                  pl.BlockSpec((8, ns), lambda b, c: (jnp.maximum((b * nc + c) * blk - 1, 0), 0)),
                  pl.BlockSpec((None, 1, ns), lambda b, c: (b, 0, 0))] + [cs(a) for a in consts]
                 + [st, cs(lp["gn_w"]), cs(lp["gn_b"])],
        out_specs=[row(dr), st],
        out_shape=[jax.ShapeDtypeStruct((n, dr), BF16), jax.ShapeDtypeStruct((batch, HEAD, dr), F32)],
        scratch_shapes=[pltpu.VMEM((HEAD, dr), F32)],
        compiler_params=_params("arbitrary", "arbitrary"),
        name="wkv",
    )(ps, ps, shift_prev, *consts, h0, lp["gn_w"], lp["gn_b"])


def _flash_init(m_scr, acc_scr):
    m_scr[...] = jnp.full(m_scr.shape, NEG_BIG, F32)
    acc_scr[...] = jnp.zeros(acc_scr.shape, F32)


def _lane_rep(x, width):
    if width % PAIR == 0:
        return x if width == PAIR else jnp.concatenate([x] * (width // PAIR), axis=1)
    return x[:, :width]


def _flash_update(q, k_of, v_of, tk, m_scr, acc_scr, mask=None):
    tq = q.shape[0]
    first = lax.broadcasted_iota(jnp.int32, (tq, PAIR), 1) < HEAD
    zq = jnp.zeros((tq, PAIR), BF16)
    ones = jnp.ones((tk, PAIR), BF16)
    n_maps = 2 * (q.shape[1] // PAIR)

    def scores(i):
        qh = q[:, (i // 2) * PAIR:(i // 2 + 1) * PAIR]
        qm = jnp.where(first, qh, zq) if i % 2 == 0 else jnp.where(first, zq, qh)
        return _dg(qm, k_of(i // 2), NT)

    s_next = scores(0)
    for i in range(n_maps):
        s = s_next
        if i + 1 < n_maps:
            s_next = scores(i + 1)
        if mask is not None:
            s = jnp.where(mask, s, NEG_BIG)
        v_aug = jnp.concatenate([v_of(i // 2), ones], axis=1)
        m_prev = m_scr[i]
        m_new = jnp.maximum(m_prev, jnp.max(s, axis=1, keepdims=True))
        alpha = jnp.exp2(m_prev - m_new)
        p = jnp.exp2((s - _lane_rep(m_new, tk)).astype(BF16))
        acc_scr[i] = _lane_rep(alpha, 2 * PAIR) * acc_scr[i] + _dg(p, v_aug, NN)
        m_scr[i] = m_new


def _flash_finish(lam, sub_ref, scale, y_ref, acc_scr):
    for h in range(y_ref.shape[1] // PAIR):
        a1, a2 = acc_scr[2 * h], acc_scr[2 * h + 1]
        o = a1[:, :PAIR] / a1[:, PAIR:] - lam * (a2[:, :PAIR] / a2[:, PAIR:])
        ms = jnp.mean(o * o, axis=-1, keepdims=True)
        y = o * lax.rsqrt(ms + NORM_EPS) * sub_ref[...] * scale
        y_ref[:, h * PAIR:(h + 1) * PAIR] = y.astype(BF16)


def _attn_prompt_body(qt_ref, kt_ref, lam_ref, q_ref, k_ref, v_ref, sub_ref, y_ref, m_scr, acc_scr, *, tq, scale):
    s = pl.program_id(1)
    qi = qt_ref[s]
    kj = kt_ref[s]
    head = lambda h: slice(h * PAIR, (h + 1) * PAIR)

    @pl.when(kj == 0)
    def _():
        _flash_init(m_scr, acc_scr)

    for j in range(k_ref.shape[0] // tq):
        rows = slice(j * tq, (j + 1) * tq)
        k_of = lambda h, rows=rows: k_ref[rows, head(h)]
        v_of = lambda h, rows=rows: v_ref[rows, head(h)]
        tile = kj * (k_ref.shape[0] // tq) + j

        @pl.when(tile < qi)
        def _(k_of=k_of, v_of=v_of):
            _flash_update(q_ref[...], k_of, v_of, tq, m_scr, acc_scr)

        @pl.when(tile == qi)
        def _(k_of=k_of, v_of=v_of):
            qrow = lax.broadcasted_iota(jnp.int32, (tq, tq), 0)
            kcol = lax.broadcasted_iota(jnp.int32, (tq, tq), 1)
            mask = (kcol // CHUNK) <= (qrow // CHUNK)
            _flash_update(q_ref[...], k_of, v_of, tq, m_scr, acc_scr, mask)
            _flash_finish(lam_ref[0], sub_ref, scale, y_ref, acc_scr)


def _attn_prompt(lam, q, kb, vb, subln, scale, batch, seq):
    n, dd = q.shape
    tq = min(512, seq)
    nq = seq // tq
    nh2 = 2 * dd // PAIR
    nkt = 2 if nq % 2 == 0 else 1
    pairs = [(i, j) for i in range(nq) for j in range(i // nkt + 1)]
    qt = jnp.array([p[0] for p in pairs], jnp.int32)
    kt = jnp.array([p[1] for p in pairs], jnp.int32)
    kv_spec = pl.BlockSpec((tq * nkt, dd), lambda b, s, qt, kt: (b * (nq // nkt) + kt[s], 0))
    q_spec = pl.BlockSpec((tq, dd), lambda b, s, qt, kt: (b * nq + qt[s], 0))
    return pl.pallas_call(
        functools.partial(_attn_prompt_body, tq=tq, scale=scale),
        grid_spec=pltpu.PrefetchScalarGridSpec(
            num_scalar_prefetch=2,
            grid=(batch, len(pairs)),
            in_specs=[pl.BlockSpec(memory_space=pltpu.SMEM), q_spec, kv_spec, kv_spec,
                      pl.BlockSpec((1, PAIR), lambda b, s, qt, kt: (0, 0))],
            out_specs=q_spec,
            scratch_shapes=[pltpu.VMEM((nh2, tq, PAIR), F32), pltpu.VMEM((nh2, tq, 2 * PAIR), F32)]),
        out_shape=jax.ShapeDtypeStruct((n, dd), BF16),
        compiler_params=_params("arbitrary", "arbitrary"),
        name="attn_prompt",
    )(qt, kt, lam, q, kb, vb, subln)


def _attn_sample_body(lam_ref, q_ref, ck_ref, cv_ref, k_ref, v_ref, sub_ref, y_ref, m_scr, acc_scr, *, scale):
    kj = pl.program_id(1)
    last = pl.num_programs(1) - 1

    @pl.when(kj == 0)
    def _():
        _flash_init(m_scr, acc_scr)

    head = lambda h: slice(h * PAIR, (h + 1) * PAIR)

    nh = q_ref.shape[1] // PAIR
    tk = ck_ref.shape[0] // nh

    @pl.when(kj < last)
    def _():
        _flash_update(q_ref[...], lambda h: ck_ref[pl.ds(h, tk, stride=nh), :].astype(BF16),
                      lambda h: cv_ref[pl.ds(h, tk, stride=nh), :].astype(BF16), tk, m_scr, acc_scr)

    @pl.when(kj == last)
    def _():
        _flash_update(q_ref[...], lambda h: k_ref[:, head(h)], lambda h: v_ref[:, head(h)],
                      k_ref.shape[0], m_scr, acc_scr)
        _flash_finish(lam_ref[0], sub_ref, scale, y_ref, acc_scr)


def _attn_sample(lam, q, cache_k, cache_v, layer, kb, vb, subln, scale, batch, seq):
    n, dd = q.shape
    past = cache_k.shape[2]
    tk = min(1024, past)
    nk = past // tk
    nh2 = 2 * dd // PAIR
    row = pl.BlockSpec((seq, dd), lambda b, j: (b, 0))
    nh = dd // PAIR
    cache_k = cache_k.reshape(cache_k.shape[:2] + (past * nh, PAIR))
    cache_v = cache_v.reshape(cache_v.shape[:2] + (past * nh, PAIR))
    cache = pl.BlockSpec((None, None, tk * nh, PAIR), lambda b, j: (layer, b, jnp.minimum(j, nk - 1), 0))
    return pl.pallas_call(
        functools.partial(_attn_sample_body, scale=scale),
        grid=(batch, nk + 1),
        in_specs=[pl.BlockSpec(memory_space=pltpu.SMEM), row, cache, cache, row, row, _const_spec((1, PAIR), 2)],
        out_specs=row,
        out_shape=jax.ShapeDtypeStruct((n, dd), BF16),
        scratch_shapes=[pltpu.VMEM((nh2, seq, PAIR), F32), pltpu.VMEM((nh2, seq, 2 * PAIR), F32)],
        compiler_params=_params("arbitrary", "arbitrary"),
        name="attn_sample",
    )(lam, q, cache_k, cache_v, kb, vb, subln)


def _ffn_body(h_ref, yr_ref, ya_ref, wo_ref, gf_ref, wup_ref, cw_ref, cb_ref, wdn_ref, cprev_ref,
              hout_ref, cnew_ref, carry_scr, *, dff, fc):
    t = pl.program_id(1)

    @pl.when(t == 0)
    def _():
        carry_scr[...] = cprev_ref[...]

    y = jnp.concatenate([yr_ref[...], ya_ref[...]], axis=1)
    h1 = h_ref[...] + _dg(y, wo_ref[...], NN)
    ms = jnp.mean(h1 * h1, axis=-1, keepdims=True)
    xn = (h1 * lax.rsqrt(ms + NORM_EPS) * gf_ref[...]).astype(BF16)
    tm = h1.shape[0]
    rowi = lax.broadcasted_iota(jnp.int32, (tm, fc), 0)
    acc = jnp.zeros(h1.shape, F32)

    def up(c):
        return (_dg(xn, wup_ref[:, c * fc:(c + 1) * fc], NN),
                _dg(xn, wup_ref[:, dff + c * fc:dff + (c + 1) * fc], NN))

    nxt = up(0)
    for c in range(dff // fc):
        sl = slice(c * fc, (c + 1) * fc)
        gt, u = nxt
        if c + 1 < dff // fc:
            nxt = up(c + 1)
        p1 = carry_scr[7:8, sl]
        p2 = carry_scr[6:7, sl]
        g1 = jnp.where(rowi == 0, p1, pltpu.roll(gt, 1, axis=0))
        g2 = jnp.where(rowi == 0, p2, jnp.where(rowi == 1, p1, pltpu.roll(gt, 2, axis=0)))
        gc = cb_ref[:, sl] + g2 * cw_ref[0:1, sl] + g1 * cw_ref[1:2, sl] + gt * cw_ref[2:3, sl]
        carry_scr[:, sl] = gt[tm - 8:, :]
        hh = (gc * jax.nn.sigmoid(gc) * u).astype(BF16)
        acc = acc + _dg(hh, wdn_ref[sl, :], NN)
    hout_ref[...] = h1 + acc
    cnew_ref[...] = carry_scr[...]


def _ffn(h, yr, ya, wo, g_ffn, wup, cw, cb, wdn, conv_prev8, batch, seq):
    n, d = h.shape
    dff = cb.shape[-1]
    fc = 256
    tm = min(512, seq)
    nt = seq // tm
    row = lambda w: pl.BlockSpec((tm, w), lambda b, t: (b * nt + t, 0))
    st = pl.BlockSpec((None, 8, dff), lambda b, t: (b, 0, 0))
    cs = lambda a: _const_spec(a.shape, 2)
    return pl.pallas_call(
        functools.partial(_ffn_body, dff=dff, fc=fc),
        grid=(batch, nt),
        in_specs=[row(d), row(yr.shape[1]), row(ya.shape[1]), cs(wo), cs(g_ffn), cs(wup), cs(cw), cs(cb), cs(wdn), st],
        out_specs=[row(d), st],
        out_shape=[jax.ShapeDtypeStruct((n, d), F32), jax.ShapeDtypeStruct((batch, 8, dff), F32)],
        scratch_shapes=[pltpu.VMEM((8, dff), F32)],
        compiler_params=_params("arbitrary", "arbitrary"),
        name="ffn",
    )(h, yr, ya, wo, g_ffn, wup, cw, cb, wdn, conv_prev8)


WKV_ROWS = 1024
WKV_SUB = 256


def _lambda_init(layer):
    return 0.8 - 0.6 * math.exp(-0.3 * layer)


def _run_group(x, depth, layers, ones_pair, shift_prev, wkv_prev, conv_prev, attend):
    batch, seq, d = x.shape
    h = x.reshape(batch * seq, d)
    ws, ss, cs = [], [], []
    nh = layers[0]["q_gain"].shape[-1] // PAIR
    k_all = jnp.zeros((depth, batch * seq * nh, PAIR), F32)
    v_all = jnp.zeros((depth, batch * seq * nh, PAIR), F32)
    for l in range(depth):
        lp = layers[l]
        ns = lp["mu"].shape[-1]
        dd = lp["q_gain"].shape[-1]
        dr = lp["w0"].shape[-1]
        dff = lp["conv_b"].shape[-1]
        ps, q, k_all, v_all, kb, vb = _inproj(h, lp["g_mix"], lp["w_in"], lp["q_gain"], lp["k_gain"],
                                              k_all, v_all, l, ns, dd)

        sp = jnp.zeros((batch, 1, ns), F32) if shift_prev is None else shift_prev[l][:, None, :]
        if wkv_prev is None:
            h0 = jnp.zeros((batch, HEAD, dr), F32)
        else:
            h0 = jnp.transpose(wkv_prev[l], (0, 3, 1, 2)).reshape(batch, HEAD, dr)
        yr, hout = _wkv(ps, sp, h0, lp, ones_pair, batch, seq)

        lam_init = _lambda_init(l)
        ya = attend(l, lp["lam"], q, kb, vb, lp["subln"], 1.0 - lam_init, batch, seq)

        cp = jnp.zeros((batch, 8, dff), F32) if conv_prev is None else jnp.pad(conv_prev[l], ((0, 0), (6, 0), (0, 0)))
        h, cnew = _ffn(h, yr, ya, lp["w_out"], lp["g_ffn"], lp["w_up"], lp["conv_w"], lp["conv_b"], lp["w_down"],
                       cp, batch, seq)

        ws.append(jnp.transpose(hout.reshape(batch, HEAD, dr // HEAD, HEAD), (0, 2, 3, 1)))
        ss.append(ps.reshape(batch, seq, ns)[:, -1])
        cs.append(cnew[:, 6:8])
    leaf = lambda t: t.reshape(depth, batch, seq, nh, PAIR)
    return (h.reshape(batch, seq, d), leaf(k_all), leaf(v_all), jnp.stack(ws), jnp.stack(ss), jnp.stack(cs))


def kernel(x_prompt, x_sample, cache_k, cache_v, state_wkv, state_shift, state_conv, g_mix, w_in, mu_shift, w0, w_decay, a0, w_aaa, w_gate, k_k, k_a, r_k, gn_w, gn_b, q_gain, k_gain, lambdas, subln_gain, w_out, g_ffn, w_ffn_in, conv_w, conv_b, w_ffn_out):
    depth = w_in.shape[0]
    dr = w0.shape[-1]
    ns = mu_shift.shape[-1]
    dd = (w_in.shape[-1] - ns) // 3
    assert dr % PAIR == 0 and dd % PAIR == 0 and q_gain.shape[-1] == HEAD and subln_gain.shape[-1] == PAIR
    assert x_prompt.shape[1] % CHUNK == 0 and x_sample.shape[1] == CHUNK

    row = lambda a: a.reshape(1, -1).astype(F32)
    layers = []
    for l in range(depth):
        lv = lambdas[l].astype(F32)
        lam = jnp.exp(jnp.sum(lv[0] * lv[1])) - jnp.exp(jnp.sum(lv[2] * lv[3])) + _lambda_init(l)
        layers.append(dict(
            g_mix=row(g_mix[l]), w_in=w_in[l].astype(BF16), mu=row(mu_shift[l]), w0=row(w0[l]), a0=row(a0[l]),
            w_decay=w_decay[l], w_aaa=w_aaa[l], w_gate=w_gate[l], k_k=row(k_k[l]), k_a=row(k_a[l]), r_k=row(r_k[l]),
            gn_w=row(gn_w[l]), gn_b=row(gn_b[l]),
            q_gain=row(jnp.tile(q_gain[l], dd // HEAD)), k_gain=row(jnp.tile(k_gain[l], dd // HEAD)),
            lam=lam.reshape(1), subln=row(subln_gain[l]), w_out=w_out[l].astype(BF16), g_ffn=row(g_ffn[l]),
            w_up=w_ffn_in[l].astype(BF16), conv_w=conv_w[l], conv_b=row(conv_b[l]), w_down=w_ffn_out[l].astype(BF16)))

    ones_pair = jnp.kron(jnp.eye(2, dtype=F32), jnp.ones((HEAD, HEAD), F32)).astype(BF16)

    def attend_prompt(l, lam, q, kb, vb, subln, scale, batch, seq):
        return _attn_prompt(lam, q, kb, vb, subln, scale, batch, seq)


    def attend_sample(l, lam, q, kb, vb, subln, scale, batch, seq):
        return _attn_sample(lam, q, cache_k, cache_v, l, kb, vb, subln, scale, batch, seq)

    yp, pk, pv, pw, ps_, pc = _run_group(x_prompt, depth, layers, ones_pair, None, None, None, attend_prompt)
    ys, sk, sv, sw, ss, sc = _run_group(x_sample, depth, layers, ones_pair, state_shift, state_wkv, state_conv,
                                        attend_sample)
    return (yp, ys, pk, pv, pw, ps_, pc, sk, sv, sw, ss, sc)
```

```python
import functools
import math

import jax
import jax.numpy as jnp
from jax import lax
from jax.experimental import pallas as pl
from jax.experimental.pallas import tpu as pltpu

F32 = jnp.float32
BF16 = jnp.bfloat16

HEAD = 64
PAIR = 2 * HEAD
CHUNK = 64
NORM_EPS = 1e-6
GN_EPS = 64e-5
NEG_BIG = -1e30
LOG2E = math.log2(math.e)
VMEM_LIMIT = 56 * 1024 * 1024

NN = (((1,), (0,)), ((), ()))
NT = (((1,), (1,)), ((), ()))
TN = (((0,), (0,)), ((), ()))


def _dg(a, b, dn):
    return lax.dot_general(a, b, dn, preferred_element_type=F32)


def _split2(x):
    hi = x.astype(BF16)
    lo = (x - hi.astype(F32)).astype(BF16)
    return hi, lo


def _mm(a, b, dn=NN):
    return _dg(a.astype(BF16), b.astype(BF16), dn)


def _mm3(a, b):
    ah, al = _split2(a)
    bh, bl = _split2(b)
    return _dg(ah, bh, NN) + (_dg(ah, bl, NN) + _dg(al, bh, NN))


def _mm_exact_rhs(a, e):
    hi, lo = _split2(a)
    return _dg(hi, e, NN) + _dg(lo, e, NN)


def _mm_exact_lhs(e, b):
    hi, lo = _split2(b)
    return _dg(e, hi, NN) + _dg(e, lo, NN)


def _segsum(x, ones_bd):
    w = ones_bd.shape[0]
    parts = [_mm_exact_rhs(x[:, i:i + w], ones_bd) for i in range(0, x.shape[1], w)]
    return parts[0] if len(parts) == 1 else jnp.concatenate(parts, axis=1)


def _params(*sem):
    return pltpu.CompilerParams(dimension_semantics=sem, vmem_limit_bytes=VMEM_LIMIT)


def _const_spec(shape, grid_rank):
    zeros = (0,) * len(shape)
    if grid_rank == 1:
        return pl.BlockSpec(shape, lambda i: zeros, pipeline_mode=pl.Buffered(1))
    if grid_rank == 2:
        return pl.BlockSpec(shape, lambda i, j: zeros, pipeline_mode=pl.Buffered(1))
    return pl.BlockSpec(shape, lambda i, j, k: zeros, pipeline_mode=pl.Buffered(1))


def _inproj_body(x_ref, g_ref, w_ref, qg_ref, kg_ref, ones_ref, kall_ref, vall_ref,
                 ps_ref, q_ref, k_ref, v_ref, kb_ref, vb_ref, *, ns, dd):
    del kall_ref, vall_ref
    x = x_ref[...]
    ms = jnp.mean(x * x, axis=-1, keepdims=True)
    xn = (x * lax.rsqrt(ms + NORM_EPS) * g_ref[...]).astype(BF16)
    ps_ref[...] = _dg(xn, w_ref[:, :ns], NN)
    ones = ones_ref[...]

    def head_norm(t, gain):
        w = ones.shape[0]
        sq = (t * t).astype(BF16)
        ss = jnp.concatenate([_dg(sq[:, i:i + w], ones, NN) for i in range(0, dd, w)], axis=1)
        return t * lax.rsqrt(ss * (1.0 / HEAD) + NORM_EPS) * gain

    q = _dg(xn, w_ref[:, ns:ns + dd], NN)
    k = _dg(xn, w_ref[:, ns + dd:ns + 2 * dd], NN)
    v = _dg(xn, w_ref[:, ns + 2 * dd:ns + 3 * dd], NN)
    nh = dd // PAIR
    tm = x.shape[0]

    def put_heads(ref, t):
        for hd in range(nh):
            ref[pl.ds(hd, tm, stride=nh), :] = t[:, hd * PAIR:(hd + 1) * PAIR]

    put_heads(v_ref, v)
    vb_ref[...] = v.astype(BF16)
    q = head_norm(q, qg_ref[...])
    q_ref[...] = (q * (HEAD ** -0.5 * LOG2E)).astype(BF16)
    k = head_norm(k, kg_ref[...])
    put_heads(k_ref, k)
    kb_ref[...] = k.astype(BF16)


def _inproj(h, g, w_bf, q_gain, k_gain, k_all, v_all, layer, ns, dd):
    n, d = h.shape
    tm = min(512, n)
    nh = dd // PAIR
    leaf = pl.BlockSpec((None, tm * nh, PAIR), lambda i: (layer, i, 0))
    anywhere = pl.BlockSpec(memory_space=pl.ANY)
    ones = jnp.kron(jnp.eye(2 * PAIR // HEAD, dtype=F32), jnp.ones((HEAD, HEAD), F32)).astype(BF16)
    row = lambda w: pl.BlockSpec((tm, w), lambda i: (i, 0))
    return pl.pallas_call(
        functools.partial(_inproj_body, ns=ns, dd=dd),
        grid=(n // tm,),
        in_specs=[row(d), _const_spec((1, d), 1), _const_spec(w_bf.shape, 1),
                  _const_spec((1, dd), 1), _const_spec((1, dd), 1), _const_spec(ones.shape, 1),
                  anywhere, anywhere],
        out_specs=[row(ns), row(dd), leaf, leaf, row(dd), row(dd)],
        out_shape=[jax.ShapeDtypeStruct((n, ns), F32), jax.ShapeDtypeStruct((n, dd), BF16),
                   jax.ShapeDtypeStruct(k_all.shape, F32), jax.ShapeDtypeStruct(v_all.shape, F32),
                   jax.ShapeDtypeStruct((n, dd), BF16), jax.ShapeDtypeStruct((n, dd), BF16)],
        input_output_aliases={6: 2, 7: 3},
        compiler_params=_params("arbitrary"),
        name="inproj",
    )(h, g, w_bf, q_gain, k_gain, ones, k_all, v_all)


def _bd2(x, m0, swap=False):
    zero = jnp.zeros_like(x)
    first = jnp.where(m0, x, zero)
    second = jnp.where(m0, zero, x)
    return jnp.concatenate([second, first] if swap else [first, second], axis=0)


def _wkv_body(ps_ref, prev_ref, sp_ref, mu_ref, w0_ref, a0_ref, kk_ref, ka_ref, rk_ref,
              wd_ref, wa_ref, wg_ref, ones_ref, tril_ref, h0_ref, gnw_ref, gnb_ref,
              y_ref, hout_ref, h_scr, *, dr, nd, na, sub):
    c = pl.program_id(1)

    @pl.when(c == 0)
    def _():
        h_scr[...] = h0_ref[...]

    ns = ps_ref.shape[1]
    ones = ones_ref[...]
    sigmoid = lambda t: 0.5 + 0.5 * jnp.tanh(0.5 * t)
    seg = lambda t: jnp.concatenate([_dg(t[:, i:i + PAIR].astype(BF16), ones, NN) for i in range(0, dr, PAIR)],
                                    axis=1)

    lane = lax.broadcasted_iota(jnp.int32, (CHUNK, PAIR), 1)
    rr = lax.broadcasted_iota(jnp.int32, (CHUNK, PAIR), 0)
    m0 = lane < HEAD
    lane_in = jnp.where(m0, lane, lane - HEAD)
    strict = rr > lane_in
    incl = rr >= lane_in
    diag = rr == lane_in
    eye2 = jnp.where(diag, 1.0, 0.0).astype(F32)
    m0w = lax.broadcasted_iota(jnp.int32, (2 * CHUNK, PAIR), 1) < HEAD
    zero = jnp.zeros((CHUNK, PAIR), F32)
    zero2 = jnp.zeros((2 * CHUNK, PAIR), F32)

    bd = lambda t, swap=False: _bd2(t, m0, swap).astype(BF16)

    def prologue(i, pro):
        rows = slice(i * sub, (i + 1) * sub)
        x = ps_ref[rows, :]
        if i == 0:
            row0 = jnp.where(c == 0, sp_ref[...], prev_ref[7:8, :])
        else:
            row0 = ps_ref[i * sub - 1:i * sub, :]
        rowi = lax.broadcasted_iota(jnp.int32, (sub, ns), 0)
        prev = jnp.where(rowi == 0, row0, pltpu.roll(x, 1, axis=0))
        xs = x + (prev - x) * mu_ref[...]
        r = xs[:, :dr]
        k = xs[:, dr:2 * dr]
        v = xs[:, 2 * dr:3 * dr]
        o = 3 * dr
        t_w = jnp.tanh(xs[:, o:o + nd])
        xa = xs[:, o + nd:o + nd + na]
        s_g = sigmoid(xs[:, o + nd + na:])
        yield
        z = _mm(t_w, wd_ref[...])
        al = _mm(xa, wa_ref[...])
        gate = _mm(s_g, wg_ref[...])
        yield
        ld = -math.exp(-0.5) * sigmoid(w0_ref[...] + z)
        a = sigmoid(a0_ref[...] + al)
        kk = k * kk_ref[...]
        k2 = k * (1.0 + (a - 1.0) * ka_ref[...])
        kk_ss = seg(kk * kk)
        bonus = seg(r * k2 * rk_ref[...])
        cum = _mm_exact_lhs(tril_ref[...], ld)
        yield
        kk = kk * lax.rsqrt(jnp.maximum(kk_ss, 1e-24))
        bv = kk * a
        e_cum = jnp.exp(cum)
        e_neg = jnp.exp(-cum)
        chunk_start = lax.broadcasted_iota(jnp.int32, cum.shape, 0) % CHUNK == 0
        e_prev = jnp.where(chunk_start, 1.0, pltpu.roll(e_cum, 1, axis=0))
        pro.update(at=-kk * e_prev, rt=r * e_cum, bt=bv * e_neg, kt=k2 * e_neg, bv=bv, k2=k2, v=v,
                   e_cum=e_cum, e_neg=e_neg, gate=gate, bonus=bonus * v)

    def algebra(pro, res):
        chains = [(slice(ch * CHUNK, (ch + 1) * CHUNK), slice(j * PAIR, (j + 1) * PAIR), (ch + 1) * CHUNK - 1)
                  for ch in range(sub // CHUNK) for j in range(dr // PAIR)]
        atp = [pro["at"][rs, sl] for rs, sl, _ in chains]
        rtp = [pro["rt"][rs, sl] for rs, sl, _ in chains]
        vp = [pro["v"][rs, sl] for rs, sl, _ in chains]
        g0, g1 = [], []
        for n, (rs, sl, _) in enumerate(chains):
            btp, ktp = pro["bt"][rs, sl], pro["kt"][rs, sl]
            lhs = jnp.concatenate([atp[n], rtp[n]], axis=0)
            g0.append(_mm(jnp.where(m0w, lhs, zero2), jnp.concatenate([btp, ktp], axis=0), NT))
            g1.append(_mm(jnp.where(m0w, zero2, lhs), jnp.concatenate([ktp, btp], axis=0), NT))
        yield
        qp = [jnp.where(strict, jnp.where(m0, a[:CHUNK], b[:CHUNK]), zero) for a, b in zip(g0, g1)]
        akmk = [jnp.concatenate([jnp.where(strict, jnp.where(m0, b[:CHUNK], a[:CHUNK]), zero),
                                 jnp.where(incl, jnp.where(m0, b[CHUNK:], a[CHUNK:]), zero)], axis=0)
                for a, b in zip(g0, g1)]
        mb = [jnp.where(incl, jnp.where(m0, a[CHUNK:], b[CHUNK:]), zero) for a, b in zip(g0, g1)]
        kv = [_mm(l, bd(t, True)) for l, t in zip(akmk, vp)]
        yield
        tm = [eye2 + t for t in qp]
        bq = [bd(t) for t in qp]
        for _ in range(5):
            qp = [_mm(t, b) for t, b in zip(qp, bq)]
            yield
            bq = [bd(t) for t in qp]
            tm = [t + _mm(t, b) for t, b in zip(tm, bq)]
            yield
        au = [_mm(t, jnp.concatenate([bd(a), bd(k[:CHUNK])], axis=1)) for t, a, k in zip(tm, atp, kv)]
        yield
        ry = [_mm(t, jnp.concatenate([bd(a[:, :PAIR]), bd(a[:, PAIR:])], axis=1)) for t, a in zip(mb, au)]
        ph = []
        for n, (rs, sl, last) in enumerate(chains):
            e_end = pro["e_cum"][last:last + 1, sl] * pro["e_neg"][rs, sl]
            rhs = jnp.concatenate([au[n], jnp.concatenate([zero, vp[n]], axis=1)], axis=0)
            ph.append(_mm(jnp.concatenate([pro["bv"][rs, sl] * e_end, pro["k2"][rs, sl] * e_end], axis=0), rhs, TN))
        yield
        res["rp"] = [(rtp[n] + ry[n][:, :PAIR]).astype(BF16) for n in range(len(chains))]
        res["y0"] = [kv[n][CHUNK:] + ry[n][:, PAIR:] for n in range(len(chains))]
        res["pc"] = [(jnp.where(m0, ph[n][:CHUNK, :PAIR], ph[n][CHUNK:, :PAIR])
                      + jnp.where(diag, pro["e_cum"][last:last + 1, sl], 0.0)).astype(BF16)
                     for n, (rs, sl, last) in enumerate(chains)]
        res["hinc"] = [jnp.where(m0, ph[n][:CHUNK, PAIR:], ph[n][CHUNK:, PAIR:]) for n in range(len(chains))]

    npair = dr // PAIR

    def scan(i, pro, res, hs):
        ys = []
        for ch in range(sub // CHUNK):
            for j in range(npair):
                n = ch * npair + j
                out = _mm(jnp.concatenate([res["rp"][n], res["pc"][n]], axis=0), _bd2(hs[j], m0))
                ys.append(res["y0"][n] + out[:CHUNK])
                hs[j] = res["hinc"][n] + out[CHUNK:]
            yield
        y = jnp.concatenate([jnp.concatenate(ys[ch * npair:(ch + 1) * npair], axis=1)
                             for ch in range(sub // CHUNK)], axis=0)
        mean = _segsum(y, ones) * (1.0 / HEAD)
        d = y - mean
        var = _segsum(d * d, ones) * (1.0 / HEAD)
        yn = d * lax.rsqrt(var + GN_EPS) * gnw_ref[...] + gnb_ref[...]
        y_ref[i * sub:(i + 1) * sub, :] = ((yn + pro["bonus"]) * pro["gate"]).astype(BF16)

    nsub = ps_ref.shape[0] // sub
    pros = [dict() for _ in range(nsub)]
    hs = [h_scr[:, j * PAIR:(j + 1) * PAIR] for j in range(npair)]
    for _ in prologue(0, pros[0]):
        pass
    behind = iter(())
    for i in range(nsub):
        ahead = prologue(i + 1, pros[i + 1]) if i + 1 < nsub else iter(())
        res = {}
        for stage, _ in enumerate(algebra(pros[i], res)):
            next(ahead if stage % 2 == 0 else behind, None)
        for _ in ahead:
            pass
        for _ in behind:
            pass
        behind = scan(i, pros[i], res, hs)
    for _ in behind:
        pass
    for j in range(npair):
        h_scr[:, j * PAIR:(j + 1) * PAIR] = hs[j]
    hout_ref[...] = h_scr[...]


def _wkv(ps, shift_prev, h0, lp, ones_pair, batch, seq):
    n, ns = ps.shape
    dr = lp["w0"].shape[-1]
    nd = lp["w_decay"].shape[0]
    na = lp["w_aaa"].shape[0]
    rows = min(WKV_ROWS, seq)
    sub = min(WKV_SUB, rows)
    nc = seq // rows
    blk = rows // 8
    tril = jnp.kron(jnp.eye(sub // CHUNK, dtype=F32), jnp.tril(jnp.ones((CHUNK, CHUNK), F32))).astype(BF16)
    row = lambda w: pl.BlockSpec((rows, w), lambda b, c: (b * nc + c, 0))
    st = pl.BlockSpec((None, HEAD, dr), lambda b, c: (b, 0, 0))
    cs = lambda a: _const_spec(a.shape, 2)
    consts = [lp["mu"], lp["w0"], lp["a0"], lp["k_k"], lp["k_a"], lp["r_k"],
              lp["w_decay"], lp["w_aaa"], lp["w_gate"], ones_pair, tril]
    return pl.pallas_call(
        functools.partial(_wkv_body, dr=dr, nd=nd, na=na, sub=sub),
        grid=(batch, nc),
        in_specs=[row(ns),
                  pl.BlockSpec((8, ns), lambda b, c: (jnp.maximum((b * nc + c) * blk - 1, 0), 0)),
                  pl.BlockSpec((None, 1, ns), lambda b, c: (b, 0, 0))] + [cs(a) for a in consts]
                 + [st, cs(lp["gn_w"]), cs(lp["gn_b"])],
        out_specs=[row(dr), st],
        out_shape=[jax.ShapeDtypeStruct((n, dr), BF16), jax.ShapeDtypeStruct((batch, HEAD, dr), F32)],
        scratch_shapes=[pltpu.VMEM((HEAD, dr), F32)],
        compiler_params=_params("arbitrary", "arbitrary"),
        name="wkv",
    )(ps, ps, shift_prev, *consts, h0, lp["gn_w"], lp["gn_b"])


def _flash_init(m_scr, acc_scr):
    m_scr[...] = jnp.full(m_scr.shape, NEG_BIG, F32)
    acc_scr[...] = jnp.zeros(acc_scr.shape, F32)


def _lane_rep(x, width):
    if width % PAIR == 0:
        return x if width == PAIR else jnp.concatenate([x] * (width // PAIR), axis=1)
    return x[:, :width]


def _flash_update(q, k_of, v_of, tk, m_scr, acc_scr, mask=None):
    tq = q.shape[0]
    first = lax.broadcasted_iota(jnp.int32, (tq, PAIR), 1) < HEAD
    zq = jnp.zeros((tq, PAIR), BF16)
    ones = jnp.ones((tk, PAIR), BF16)
    n_maps = 2 * (q.shape[1] // PAIR)

    def scores(i):
        qh = q[:, (i // 2) * PAIR:(i // 2 + 1) * PAIR]
        qm = jnp.where(first, qh, zq) if i % 2 == 0 else jnp.where(first, zq, qh)
        return _dg(qm, k_of(i // 2), NT)

    s_next = scores(0)
    for i in range(n_maps):
        s = s_next
        if i + 1 < n_maps:
            s_next = scores(i + 1)
        if mask is not None:
            s = jnp.where(mask, s, NEG_BIG)
        v_aug = jnp.concatenate([v_of(i // 2), ones], axis=1)
        m_prev = m_scr[i]
        m_new = jnp.maximum(m_prev, jnp.max(s, axis=1, keepdims=True))
        alpha = jnp.exp2(m_prev - m_new)
        p = jnp.exp2((s - _lane_rep(m_new, tk)).astype(BF16))
        acc_scr[i] = _lane_rep(alpha, 2 * PAIR) * acc_scr[i] + _dg(p, v_aug, NN)
        m_scr[i] = m_new


def _flash_finish(lam, sub_ref, scale, y_ref, acc_scr):
    for h in range(y_ref.shape[1] // PAIR):
        a1, a2 = acc_scr[2 * h], acc_scr[2 * h + 1]
        o = a1[:, :PAIR] / a1[:, PAIR:] - lam * (a2[:, :PAIR] / a2[:, PAIR:])
        ms = jnp.mean(o * o, axis=-1, keepdims=True)
        y = o * lax.rsqrt(ms + NORM_EPS) * sub_ref[...] * scale
        y_ref[:, h * PAIR:(h + 1) * PAIR] = y.astype(BF16)


def _attn_prompt_body(qt_ref, kt_ref, lam_ref, q_ref, k_ref, v_ref, sub_ref, y_ref, m_scr, acc_scr, *, tq, scale):
    s = pl.program_id(1)
    qi = qt_ref[s]
    kj = kt_ref[s]
    head = lambda h: slice(h * PAIR, (h + 1) * PAIR)

    @pl.when(kj == 0)
    def _():
        _flash_init(m_scr, acc_scr)

    nkt = k_ref.shape[0] // tq
    last_tile = kj * nkt + nkt - 1

    @pl.when(last_tile < qi)
    def _():
        _flash_update(q_ref[...], lambda h: k_ref[:, head(h)], lambda h: v_ref[:, head(h)], nkt * tq,
                      m_scr, acc_scr)

    for j in range(nkt):
        rows = slice(j * tq, (j + 1) * tq)
        k_of = lambda h, rows=rows: k_ref[rows, head(h)]
        v_of = lambda h, rows=rows: v_ref[rows, head(h)]
        tile = kj * nkt + j

        if j < nkt - 1:
            @pl.when((last_tile >= qi) & (tile < qi))
            def _(k_of=k_of, v_of=v_of):
                _flash_update(q_ref[...], k_of, v_of, tq, m_scr, acc_scr)

        @pl.when(tile == qi)
        def _(k_of=k_of, v_of=v_of):
            qrow = lax.broadcasted_iota(jnp.int32, (tq, tq), 0)
            kcol = lax.broadcasted_iota(jnp.int32, (tq, tq), 1)
            mask = (kcol // CHUNK) <= (qrow // CHUNK)
            _flash_update(q_ref[...], k_of, v_of, tq, m_scr, acc_scr, mask)
            _flash_finish(lam_ref[0], sub_ref, scale, y_ref, acc_scr)


def _attn_prompt(lam, q, kb, vb, subln, scale, batch, seq):
    n, dd = q.shape
    tq = min(512, seq)
    nq = seq // tq
    nh2 = 2 * dd // PAIR
    nkt = 2 if nq % 2 == 0 else 1
    pairs = [(i, j) for i in range(nq) for j in range(i // nkt + 1)]
    qt = jnp.array([p[0] for p in pairs], jnp.int32)
    kt = jnp.array([p[1] for p in pairs], jnp.int32)
    kv_spec = pl.BlockSpec((tq * nkt, dd), lambda b, s, qt, kt: (b * (nq // nkt) + kt[s], 0))
    q_spec = pl.BlockSpec((tq, dd), lambda b, s, qt, kt: (b * nq + qt[s], 0))
    return pl.pallas_call(
        functools.partial(_attn_prompt_body, tq=tq, scale=scale),
        grid_spec=pltpu.PrefetchScalarGridSpec(
            num_scalar_prefetch=2,
            grid=(batch, len(pairs)),
            in_specs=[pl.BlockSpec(memory_space=pltpu.SMEM), q_spec, kv_spec, kv_spec,
                      pl.BlockSpec((1, PAIR), lambda b, s, qt, kt: (0, 0))],
            out_specs=q_spec,
            scratch_shapes=[pltpu.VMEM((nh2, tq, PAIR), F32), pltpu.VMEM((nh2, tq, 2 * PAIR), F32)]),
        out_shape=jax.ShapeDtypeStruct((n, dd), BF16),
        compiler_params=_params("arbitrary", "arbitrary"),
        name="attn_prompt",
    )(qt, kt, lam, q, kb, vb, subln)


def _attn_sample_body(lam_ref, q_ref, ck_ref, cv_ref, k_ref, v_ref, sub_ref, y_ref, m_scr, acc_scr, *, scale):
    kj = pl.program_id(1)
    last = pl.num_programs(1) - 1

    @pl.when(kj == 0)
    def _():
        _flash_init(m_scr, acc_scr)

    head = lambda h: slice(h * PAIR, (h + 1) * PAIR)

    nh = q_ref.shape[1] // PAIR
    tk = ck_ref.shape[0] // nh

    @pl.when(kj < last)
    def _():
        _flash_update(q_ref[...], lambda h: ck_ref[pl.ds(h, tk, stride=nh), :].astype(BF16),
                      lambda h: cv_ref[pl.ds(h, tk, stride=nh), :].astype(BF16), tk, m_scr, acc_scr)

    @pl.when(kj == last)
    def _():
        _flash_update(q_ref[...], lambda h: k_ref[:, head(h)], lambda h: v_ref[:, head(h)],
                      k_ref.shape[0], m_scr, acc_scr)
        _flash_finish(lam_ref[0], sub_ref, scale, y_ref, acc_scr)


def _attn_sample(lam, q, cache_k, cache_v, layer, kb, vb, subln, scale, batch, seq):
    n, dd = q.shape
    past = cache_k.shape[2]
    tk = min(1024, past)
    nk = past // tk
    nh2 = 2 * dd // PAIR
    row = pl.BlockSpec((seq, dd), lambda b, j: (b, 0))
    nh = dd // PAIR
    cache_k = cache_k.reshape(cache_k.shape[:2] + (past * nh, PAIR))
    cache_v = cache_v.reshape(cache_v.shape[:2] + (past * nh, PAIR))
    cache = pl.BlockSpec((None, None, tk * nh, PAIR), lambda b, j: (layer, b, jnp.minimum(j, nk - 1), 0))
    return pl.pallas_call(
        functools.partial(_attn_sample_body, scale=scale),
        grid=(batch, nk + 1),
        in_specs=[pl.BlockSpec(memory_space=pltpu.SMEM), row, cache, cache, row, row, _const_spec((1, PAIR), 2)],
        out_specs=row,
        out_shape=jax.ShapeDtypeStruct((n, dd), BF16),
        scratch_shapes=[pltpu.VMEM((nh2, seq, PAIR), F32), pltpu.VMEM((nh2, seq, 2 * PAIR), F32)],
        compiler_params=_params("arbitrary", "arbitrary"),
        name="attn_sample",
    )(lam, q, cache_k, cache_v, kb, vb, subln)


def _ffn_body(h_ref, yr_ref, ya_ref, wo_ref, gf_ref, wup_ref, cw_ref, cb_ref, wdn_ref, cprev_ref,
              hout_ref, cnew_ref, carry_scr, *, dff, fc):
    t = pl.program_id(1)

    @pl.when(t == 0)
    def _():
        carry_scr[...] = cprev_ref[...]

    y = jnp.concatenate([yr_ref[...], ya_ref[...]], axis=1)
    h1 = h_ref[...] + _dg(y, wo_ref[...], NN)
    ms = jnp.mean(h1 * h1, axis=-1, keepdims=True)
    xn = (h1 * lax.rsqrt(ms + NORM_EPS) * gf_ref[...]).astype(BF16)
    tm = h1.shape[0]
    rowi = lax.broadcasted_iota(jnp.int32, (tm, fc), 0)
    acc = jnp.zeros(h1.shape, F32)

    def up(c):
        return (_dg(xn, wup_ref[:, c * fc:(c + 1) * fc], NN),
                _dg(xn, wup_ref[:, dff + c * fc:dff + (c + 1) * fc], NN))

    nxt = up(0)
    hs = []
    for c in range(dff // fc):
        sl = slice(c * fc, (c + 1) * fc)
        gt, u = nxt
        if c + 1 < dff // fc:
            nxt = up(c + 1)
        p1 = carry_scr[7:8, sl]
        p2 = carry_scr[6:7, sl]
        g1 = jnp.where(rowi == 0, p1, pltpu.roll(gt, 1, axis=0))
        g2 = jnp.where(rowi == 0, p2, jnp.where(rowi == 1, p1, pltpu.roll(gt, 2, axis=0)))
        gc = cb_ref[:, sl] + g2 * cw_ref[0:1, sl] + g1 * cw_ref[1:2, sl] + gt * cw_ref[2:3, sl]
        carry_scr[:, sl] = gt[tm - 8:, :]
        hs.append((gc * jax.nn.sigmoid(gc) * u).astype(BF16))
        if len(hs) == DOWN_GROUP or c + 1 == dff // fc:
            lo = (c + 1 - len(hs)) * fc
            acc = acc + _dg(jnp.concatenate(hs, axis=1), wdn_ref[lo:(c + 1) * fc, :], NN)
            hs = []
    hout_ref[...] = h1 + acc
    cnew_ref[...] = carry_scr[...]


def _ffn(h, yr, ya, wo, g_ffn, wup, cw, cb, wdn, conv_prev8, batch, seq):
    n, d = h.shape
    dff = cb.shape[-1]
    fc = 256
    tm = min(512, seq)
    nt = seq // tm
    row = lambda w: pl.BlockSpec((tm, w), lambda b, t: (b * nt + t, 0))
    st = pl.BlockSpec((None, 8, dff), lambda b, t: (b, 0, 0))
    cs = lambda a: _const_spec(a.shape, 2)
    return pl.pallas_call(
        functools.partial(_ffn_body, dff=dff, fc=fc),
        grid=(batch, nt),
        in_specs=[row(d), row(yr.shape[1]), row(ya.shape[1]), cs(wo), cs(g_ffn), cs(wup), cs(cw), cs(cb), cs(wdn), st],
        out_specs=[row(d), st],
        out_shape=[jax.ShapeDtypeStruct((n, d), F32), jax.ShapeDtypeStruct((batch, 8, dff), F32)],
        scratch_shapes=[pltpu.VMEM((8, dff), F32)],
        compiler_params=_params("arbitrary", "arbitrary"),
        name="ffn",
    )(h, yr, ya, wo, g_ffn, wup, cw, cb, wdn, conv_prev8)


DOWN_GROUP = 4
WKV_ROWS = 1024
WKV_SUB = 256


def _lambda_init(layer):
    return 0.8 - 0.6 * math.exp(-0.3 * layer)


def _run_group(x, depth, layers, ones_pair, shift_prev, wkv_prev, conv_prev, attend):
    batch, seq, d = x.shape
    h = x.reshape(batch * seq, d)
    ws, ss, cs = [], [], []
    nh = layers[0]["q_gain"].shape[-1] // PAIR
    k_all = jnp.zeros((depth, batch * seq * nh, PAIR), F32)
    v_all = jnp.zeros((depth, batch * seq * nh, PAIR), F32)
    for l in range(depth):
        lp = layers[l]
        ns = lp["mu"].shape[-1]
        dd = lp["q_gain"].shape[-1]
        dr = lp["w0"].shape[-1]
        dff = lp["conv_b"].shape[-1]
        ps, q, k_all, v_all, kb, vb = _inproj(h, lp["g_mix"], lp["w_in"], lp["q_gain"], lp["k_gain"],
                                              k_all, v_all, l, ns, dd)

        sp = jnp.zeros((batch, 1, ns), F32) if shift_prev is None else shift_prev[l][:, None, :]
        if wkv_prev is None:
            h0 = jnp.zeros((batch, HEAD, dr), F32)
        else:
            h0 = jnp.transpose(wkv_prev[l], (0, 3, 1, 2)).reshape(batch, HEAD, dr)
        yr, hout = _wkv(ps, sp, h0, lp, ones_pair, batch, seq)

        lam_init = _lambda_init(l)
        ya = attend(l, lp["lam"], q, kb, vb, lp["subln"], 1.0 - lam_init, batch, seq)

        cp = jnp.zeros((batch, 8, dff), F32) if conv_prev is None else jnp.pad(conv_prev[l], ((0, 0), (6, 0), (0, 0)))
        h, cnew = _ffn(h, yr, ya, lp["w_out"], lp["g_ffn"], lp["w_up"], lp["conv_w"], lp["conv_b"], lp["w_down"],
                       cp, batch, seq)

        ws.append(jnp.transpose(hout.reshape(batch, HEAD, dr // HEAD, HEAD), (0, 2, 3, 1)))
        ss.append(ps.reshape(batch, seq, ns)[:, -1])
        cs.append(cnew[:, 6:8])
    leaf = lambda t: t.reshape(depth, batch, seq, nh, PAIR)
    return (h.reshape(batch, seq, d), leaf(k_all), leaf(v_all), jnp.stack(ws), jnp.stack(ss), jnp.stack(cs))


def kernel(x_prompt, x_sample, cache_k, cache_v, state_wkv, state_shift, state_conv, g_mix, w_in, mu_shift, w0, w_decay, a0, w_aaa, w_gate, k_k, k_a, r_k, gn_w, gn_b, q_gain, k_gain, lambdas, subln_gain, w_out, g_ffn, w_ffn_in, conv_w, conv_b, w_ffn_out):
    depth = w_in.shape[0]
    dr = w0.shape[-1]
    ns = mu_shift.shape[-1]
    dd = (w_in.shape[-1] - ns) // 3
    assert dr % PAIR == 0 and dd % PAIR == 0 and q_gain.shape[-1] == HEAD and subln_gain.shape[-1] == PAIR
    assert x_prompt.shape[1] % CHUNK == 0 and x_sample.shape[1] == CHUNK

    row = lambda a: a.reshape(1, -1).astype(F32)
    layers = []
    for l in range(depth):
        lv = lambdas[l].astype(F32)
        lam = jnp.exp(jnp.sum(lv[0] * lv[1])) - jnp.exp(jnp.sum(lv[2] * lv[3])) + _lambda_init(l)
        layers.append(dict(
            g_mix=row(g_mix[l]), w_in=w_in[l].astype(BF16), mu=row(mu_shift[l]), w0=row(w0[l]), a0=row(a0[l]),
            w_decay=w_decay[l], w_aaa=w_aaa[l], w_gate=w_gate[l], k_k=row(k_k[l]), k_a=row(k_a[l]), r_k=row(r_k[l]),
            gn_w=row(gn_w[l]), gn_b=row(gn_b[l]),
            q_gain=row(jnp.tile(q_gain[l], dd // HEAD)), k_gain=row(jnp.tile(k_gain[l], dd // HEAD)),
            lam=lam.reshape(1), subln=row(subln_gain[l]), w_out=w_out[l].astype(BF16), g_ffn=row(g_ffn[l]),
            w_up=w_ffn_in[l].astype(BF16), conv_w=conv_w[l], conv_b=row(conv_b[l]), w_down=w_ffn_out[l].astype(BF16)))

    ones_pair = jnp.kron(jnp.eye(2, dtype=F32), jnp.ones((HEAD, HEAD), F32)).astype(BF16)

    def attend_prompt(l, lam, q, kb, vb, subln, scale, batch, seq):
        return _attn_prompt(lam, q, kb, vb, subln, scale, batch, seq)


    def attend_sample(l, lam, q, kb, vb, subln, scale, batch, seq):
        return _attn_sample(lam, q, cache_k, cache_v, l, kb, vb, subln, scale, batch, seq)

    yp, pk, pv, pw, ps_, pc = _run_group(x_prompt, depth, layers, ones_pair, None, None, None, attend_prompt)
    ys, sk, sv, sw, ss, sc = _run_group(x_sample, depth, layers, ones_pair, state_shift, state_wkv, state_conv,
                                        attend_sample)
    return (yp, ys, pk, pv, pw, ps_, pc, sk, sv, sw, ss, sc)
```

```python
import functools
import math

import jax
import jax.numpy as jnp
from jax import lax
from jax.experimental import pallas as pl
from jax.experimental.pallas import tpu as pltpu

F32 = jnp.float32
BF16 = jnp.bfloat16

HEAD = 64
PAIR = 2 * HEAD
CHUNK = 64
NORM_EPS = 1e-6
GN_EPS = 64e-5
NEG_BIG = -1e30
LOG2E = math.log2(math.e)
VMEM_LIMIT = 56 * 1024 * 1024

NN = (((1,), (0,)), ((), ()))
NT = (((1,), (1,)), ((), ()))
TN = (((0,), (0,)), ((), ()))


def _dg(a, b, dn):
    return lax.dot_general(a, b, dn, preferred_element_type=F32)


def _split2(x):
    hi = x.astype(BF16)
    lo = (x - hi.astype(F32)).astype(BF16)
    return hi, lo


def _mm(a, b, dn=NN):
    return _dg(a.astype(BF16), b.astype(BF16), dn)


def _mm3(a, b):
    ah, al = _split2(a)
    bh, bl = _split2(b)
    return _dg(ah, bh, NN) + (_dg(ah, bl, NN) + _dg(al, bh, NN))


def _mm_exact_rhs(a, e):
    hi, lo = _split2(a)
    return _dg(hi, e, NN) + _dg(lo, e, NN)


def _mm_exact_lhs(e, b):
    hi, lo = _split2(b)
    return _dg(e, hi, NN) + _dg(e, lo, NN)


def _segsum(x, ones_bd):
    w = ones_bd.shape[0]
    parts = [_mm_exact_rhs(x[:, i:i + w], ones_bd) for i in range(0, x.shape[1], w)]
    return parts[0] if len(parts) == 1 else jnp.concatenate(parts, axis=1)


def _params(*sem):
    return pltpu.CompilerParams(dimension_semantics=sem, vmem_limit_bytes=VMEM_LIMIT)


def _const_spec(shape, grid_rank):
    zeros = (0,) * len(shape)
    if grid_rank == 1:
        return pl.BlockSpec(shape, lambda i: zeros, pipeline_mode=pl.Buffered(1))
    if grid_rank == 2:
        return pl.BlockSpec(shape, lambda i, j: zeros, pipeline_mode=pl.Buffered(1))
    return pl.BlockSpec(shape, lambda i, j, k: zeros, pipeline_mode=pl.Buffered(1))


def _inproj_body(x_ref, g_ref, w_ref, qg_ref, kg_ref, ones_ref, kall_ref, vall_ref,
                 ps_ref, q_ref, k_ref, v_ref, kb_ref, vb_ref, *, ns, dd):
    del kall_ref, vall_ref
    x = x_ref[...]
    ms = jnp.mean(x * x, axis=-1, keepdims=True)
    xn = (x * lax.rsqrt(ms + NORM_EPS) * g_ref[...]).astype(BF16)
    ps_ref[...] = _dg(xn, w_ref[:, :ns], NN)
    ones = ones_ref[...]

    def head_norm(t, gain):
        w = ones.shape[0]
        sq = (t * t).astype(BF16)
        ss = jnp.concatenate([_dg(sq[:, i:i + w], ones, NN) for i in range(0, dd, w)], axis=1)
        return t * lax.rsqrt(ss * (1.0 / HEAD) + NORM_EPS) * gain

    q = _dg(xn, w_ref[:, ns:ns + dd], NN)
    k = _dg(xn, w_ref[:, ns + dd:ns + 2 * dd], NN)
    v = _dg(xn, w_ref[:, ns + 2 * dd:ns + 3 * dd], NN)
    nh = dd // PAIR
    tm = x.shape[0]

    def put_heads(ref, t):
        for hd in range(nh):
            ref[pl.ds(hd, tm, stride=nh), :] = t[:, hd * PAIR:(hd + 1) * PAIR]

    put_heads(v_ref, v)
    vb_ref[...] = v.astype(BF16)
    q = head_norm(q, qg_ref[...])
    q_ref[...] = (q * (HEAD ** -0.5 * LOG2E)).astype(BF16)
    k = head_norm(k, kg_ref[...])
    put_heads(k_ref, k)
    kb_ref[...] = k.astype(BF16)


def _inproj(h, g, w_bf, q_gain, k_gain, k_all, v_all, layer, ns, dd):
    n, d = h.shape
    tm = min(512, n)
    nh = dd // PAIR
    leaf = pl.BlockSpec((None, tm * nh, PAIR), lambda i: (layer, i, 0))
    anywhere = pl.BlockSpec(memory_space=pl.ANY)
    ones = jnp.kron(jnp.eye(2 * PAIR // HEAD, dtype=F32), jnp.ones((HEAD, HEAD), F32)).astype(BF16)
    row = lambda w: pl.BlockSpec((tm, w), lambda i: (i, 0))
    return pl.pallas_call(
        functools.partial(_inproj_body, ns=ns, dd=dd),
        grid=(n // tm,),
        in_specs=[row(d), _const_spec((1, d), 1), _const_spec(w_bf.shape, 1),
                  _const_spec((1, dd), 1), _const_spec((1, dd), 1), _const_spec(ones.shape, 1),
                  anywhere, anywhere],
        out_specs=[row(ns), row(dd), leaf, leaf, row(dd), row(dd)],
        out_shape=[jax.ShapeDtypeStruct((n, ns), F32), jax.ShapeDtypeStruct((n, dd), BF16),
                   jax.ShapeDtypeStruct(k_all.shape, F32), jax.ShapeDtypeStruct(v_all.shape, F32),
                   jax.ShapeDtypeStruct((n, dd), BF16), jax.ShapeDtypeStruct((n, dd), BF16)],
        input_output_aliases={6: 2, 7: 3},
        compiler_params=_params("arbitrary"),
        name="inproj",
    )(h, g, w_bf, q_gain, k_gain, ones, k_all, v_all)


def _bd2(x, m0, swap=False):
    zero = jnp.zeros_like(x)
    first = jnp.where(m0, x, zero)
    second = jnp.where(m0, zero, x)
    return jnp.concatenate([second, first] if swap else [first, second], axis=0)


def _wkv_body(ps_ref, prev_ref, sp_ref, mu_ref, w0_ref, a0_ref, kk_ref, ka_ref, rk_ref,
              wd_ref, wa_ref, wg_ref, ones_ref, tril_ref, h0_ref, gnw_ref, gnb_ref,
              y_ref, hout_ref, h_scr, *, dr, nd, na, sub):
    c = pl.program_id(1)

    @pl.when(c == 0)
    def _():
        h_scr[...] = h0_ref[...]

    ns = ps_ref.shape[1]
    ones = ones_ref[...]
    sigmoid = lambda t: 0.5 + 0.5 * jnp.tanh(0.5 * t)
    seg = lambda t: jnp.concatenate([_dg(t[:, i:i + PAIR].astype(BF16), ones, NN) for i in range(0, dr, PAIR)],
                                    axis=1)

    lane = lax.broadcasted_iota(jnp.int32, (CHUNK, PAIR), 1)
    rr = lax.broadcasted_iota(jnp.int32, (CHUNK, PAIR), 0)
    m0 = lane < HEAD
    lane_in = jnp.where(m0, lane, lane - HEAD)
    strict = rr > lane_in
    incl = rr >= lane_in
    diag = rr == lane_in
    eye2 = jnp.where(diag, 1.0, 0.0).astype(F32)
    m0w = lax.broadcasted_iota(jnp.int32, (2 * CHUNK, PAIR), 1) < HEAD
    zero = jnp.zeros((CHUNK, PAIR), F32)
    zero2 = jnp.zeros((2 * CHUNK, PAIR), F32)

    bd = lambda t, swap=False: _bd2(t, m0, swap).astype(BF16)

    def prologue(i, pro):
        rows = slice(i * sub, (i + 1) * sub)
        x = ps_ref[rows, :]
        if i == 0:
            row0 = jnp.where(c == 0, sp_ref[...], prev_ref[7:8, :])
        else:
            row0 = ps_ref[i * sub - 1:i * sub, :]
        rowi = lax.broadcasted_iota(jnp.int32, (sub, ns), 0)
        prev = jnp.where(rowi == 0, row0, pltpu.roll(x, 1, axis=0))
        xs = x + (prev - x) * mu_ref[...]
        r = xs[:, :dr]
        k = xs[:, dr:2 * dr]
        v = xs[:, 2 * dr:3 * dr]
        o = 3 * dr
        t_w = jnp.tanh(xs[:, o:o + nd])
        xa = xs[:, o + nd:o + nd + na]
        s_g = sigmoid(xs[:, o + nd + na:])
        yield
        z = _mm(t_w, wd_ref[...])
        al = _mm(xa, wa_ref[...])
        gate = _mm(s_g, wg_ref[...])
        yield
        ld = -math.exp(-0.5) * sigmoid(w0_ref[...] + z)
        a = sigmoid(a0_ref[...] + al)
        kk = k * kk_ref[...]
        k2 = k * (1.0 + (a - 1.0) * ka_ref[...])
        kk_ss = seg(kk * kk)
        bonus = seg(r * k2 * rk_ref[...])
        cum = _mm_exact_lhs(tril_ref[...], ld)
        yield
        kk = kk * lax.rsqrt(jnp.maximum(kk_ss, 1e-24))
        bv = kk * a
        e_cum = jnp.exp(cum)
        e_neg = jnp.exp(-cum)
        chunk_start = lax.broadcasted_iota(jnp.int32, cum.shape, 0) % CHUNK == 0
        e_prev = jnp.where(chunk_start, 1.0, pltpu.roll(e_cum, 1, axis=0))
        pro.update(at=-kk * e_prev, rt=r * e_cum, bt=bv * e_neg, kt=k2 * e_neg, bv=bv, k2=k2, v=v,
                   e_cum=e_cum, e_neg=e_neg, gate=gate, bonus=bonus * v)

    def algebra(pro, res):
        chains = [(slice(ch * CHUNK, (ch + 1) * CHUNK), slice(j * PAIR, (j + 1) * PAIR), (ch + 1) * CHUNK - 1)
                  for ch in range(sub // CHUNK) for j in range(dr // PAIR)]
        atp = [pro["at"][rs, sl] for rs, sl, _ in chains]
        rtp = [pro["rt"][rs, sl] for rs, sl, _ in chains]
        vp = [pro["v"][rs, sl] for rs, sl, _ in chains]
        g0, g1 = [], []
        for n, (rs, sl, _) in enumerate(chains):
            btp, ktp = pro["bt"][rs, sl], pro["kt"][rs, sl]
            lhs = jnp.concatenate([atp[n], rtp[n]], axis=0)
            g0.append(_mm(jnp.where(m0w, lhs, zero2), jnp.concatenate([btp, ktp], axis=0), NT))
            g1.append(_mm(jnp.where(m0w, zero2, lhs), jnp.concatenate([ktp, btp], axis=0), NT))
        yield
        qp = [jnp.where(strict, jnp.where(m0, a[:CHUNK], b[:CHUNK]), zero) for a, b in zip(g0, g1)]
        akmk = [jnp.concatenate([jnp.where(strict, jnp.where(m0, b[:CHUNK], a[:CHUNK]), zero),
                                 jnp.where(incl, jnp.where(m0, b[CHUNK:], a[CHUNK:]), zero)], axis=0)
                for a, b in zip(g0, g1)]
        mb = [jnp.where(incl, jnp.where(m0, a[CHUNK:], b[CHUNK:]), zero) for a, b in zip(g0, g1)]
        kv = [_mm(l, bd(t, True)) for l, t in zip(akmk, vp)]
        yield
        tm = [eye2 + t for t in qp]
        bq = [bd(t) for t in qp]
        for _ in range(5):
            qp = [_mm(t, b) for t, b in zip(qp, bq)]
            yield
            bq = [bd(t) for t in qp]
            tm = [t + _mm(t, b) for t, b in zip(tm, bq)]
            yield
        au = [_mm(t, jnp.concatenate([bd(a), bd(k[:CHUNK])], axis=1)) for t, a, k in zip(tm, atp, kv)]
        yield
        ry = [_mm(t, jnp.concatenate([bd(a[:, :PAIR]), bd(a[:, PAIR:])], axis=1)) for t, a in zip(mb, au)]
        ph = []
        for n, (rs, sl, last) in enumerate(chains):
            e_end = pro["e_cum"][last:last + 1, sl] * pro["e_neg"][rs, sl]
            rhs = jnp.concatenate([au[n], jnp.concatenate([zero, vp[n]], axis=1)], axis=0)
            ph.append(_mm(jnp.concatenate([pro["bv"][rs, sl] * e_end, pro["k2"][rs, sl] * e_end], axis=0), rhs, TN))
        yield
        res["rp"] = [(rtp[n] + ry[n][:, :PAIR]).astype(BF16) for n in range(len(chains))]
        res["y0"] = [kv[n][CHUNK:] + ry[n][:, PAIR:] for n in range(len(chains))]
        res["pc"] = [(jnp.where(m0, ph[n][:CHUNK, :PAIR], ph[n][CHUNK:, :PAIR])
                      + jnp.where(diag, pro["e_cum"][last:last + 1, sl], 0.0)).astype(BF16)
                     for n, (rs, sl, last) in enumerate(chains)]
        res["hinc"] = [jnp.where(m0, ph[n][:CHUNK, PAIR:], ph[n][CHUNK:, PAIR:]) for n in range(len(chains))]

    npair = dr // PAIR

    def scan(i, pro, res, hs):
        ys = []
        for ch in range(sub // CHUNK):
            for j in range(npair):
                n = ch * npair + j
                out = _mm(jnp.concatenate([res["rp"][n], res["pc"][n]], axis=0), _bd2(hs[j], m0))
                ys.append(res["y0"][n] + out[:CHUNK])
                hs[j] = res["hinc"][n] + out[CHUNK:]
            yield
        y = jnp.concatenate([jnp.concatenate(ys[ch * npair:(ch + 1) * npair], axis=1)
                             for ch in range(sub // CHUNK)], axis=0)
        mean = _segsum(y, ones) * (1.0 / HEAD)
        d = y - mean
        var = _segsum(d * d, ones) * (1.0 / HEAD)
        yn = d * lax.rsqrt(var + GN_EPS) * gnw_ref[...] + gnb_ref[...]
        y_ref[i * sub:(i + 1) * sub, :] = ((yn + pro["bonus"]) * pro["gate"]).astype(BF16)

    nsub = ps_ref.shape[0] // sub
    pros = [dict() for _ in range(nsub)]
    hs = [h_scr[:, j * PAIR:(j + 1) * PAIR] for j in range(npair)]
    for _ in prologue(0, pros[0]):
        pass
    behind = iter(())
    for i in range(nsub):
        ahead = prologue(i + 1, pros[i + 1]) if i + 1 < nsub else iter(())
        res = {}
        for stage, _ in enumerate(algebra(pros[i], res)):
            next(ahead if stage % 2 == 0 else behind, None)
        for _ in ahead:
            pass
        for _ in behind:
            pass
        behind = scan(i, pros[i], res, hs)
    for _ in behind:
        pass
    for j in range(npair):
        h_scr[:, j * PAIR:(j + 1) * PAIR] = hs[j]
    hout_ref[...] = h_scr[...]


def _wkv(ps, shift_prev, h0, lp, ones_pair, batch, seq):
    n, ns = ps.shape
    dr = lp["w0"].shape[-1]
    nd = lp["w_decay"].shape[0]
    na = lp["w_aaa"].shape[0]
    rows = min(WKV_ROWS, seq)
    sub = min(WKV_SUB, rows)
    nc = seq // rows
    blk = rows // 8
    tril = jnp.kron(jnp.eye(sub // CHUNK, dtype=F32), jnp.tril(jnp.ones((CHUNK, CHUNK), F32))).astype(BF16)
    row = lambda w: pl.BlockSpec((rows, w), lambda b, c: (b * nc + c, 0))
    st = pl.BlockSpec((None, HEAD, dr), lambda b, c: (b, 0, 0))
    cs = lambda a: _const_spec(a.shape, 2)
    consts = [lp["mu"], lp["w0"], lp["a0"], lp["k_k"], lp["k_a"], lp["r_k"],
              lp["w_decay"], lp["w_aaa"], lp["w_gate"], ones_pair, tril]
    return pl.pallas_call(
        functools.partial(_wkv_body, dr=dr, nd=nd, na=na, sub=sub),
        grid=(batch, nc),
        in_specs=[row(ns),
                  pl.BlockSpec((8, ns), lambda b, c: (jnp.maximum((b * nc + c) * blk - 1, 0), 0)),
                  pl.BlockSpec((None, 1, ns), lambda b, c: (b, 0, 0))] + [cs(a) for a in consts]
                 + [st, cs(lp["gn_w"]), cs(lp["gn_b"])],
        out_specs=[row(dr), st],
        out_shape=[jax.ShapeDtypeStruct((n, dr), BF16), jax.ShapeDtypeStruct((batch, HEAD, dr), F32)],
        scratch_shapes=[pltpu.VMEM((HEAD, dr), F32)],
        compiler_params=_params("arbitrary", "arbitrary"),
        name="wkv",
    )(ps, ps, shift_prev, *consts, h0, lp["gn_w"], lp["gn_b"])


def _flash_init(m_scr, acc_scr):
    m_scr[...] = jnp.full(m_scr.shape, NEG_BIG, F32)
    acc_scr[...] = jnp.zeros(acc_scr.shape, F32)


def _lane_rep(x, width):
    if width % PAIR == 0:
        return x if width == PAIR else jnp.concatenate([x] * (width // PAIR), axis=1)
    return x[:, :width]


def _flash_update(q, k_of, v_of, tk, m_scr, acc_scr, mask=None, rows=slice(None)):
    tq = q.shape[0]
    first = lax.broadcasted_iota(jnp.int32, (tq, PAIR), 1) < HEAD
    zq = jnp.zeros((tq, PAIR), BF16)
    ones = jnp.ones((tk, PAIR), BF16)
    n_maps = 2 * (q.shape[1] // PAIR)

    def scores(i):
        qh = q[:, (i // 2) * PAIR:(i // 2 + 1) * PAIR]
        qm = jnp.where(first, qh, zq) if i % 2 == 0 else jnp.where(first, zq, qh)
        return _dg(qm, k_of(i // 2), NT)

    s_next = scores(0)
    for i in range(n_maps):
        s = s_next
        if i + 1 < n_maps:
            s_next = scores(i + 1)
        if mask is not None:
            s = jnp.where(mask, s, NEG_BIG)
        v_aug = jnp.concatenate([v_of(i // 2), ones], axis=1)
        m_prev = m_scr[i, rows, :]
        m_new = jnp.maximum(m_prev, jnp.max(s, axis=1, keepdims=True))
        alpha = jnp.exp2(m_prev - m_new)
        p = jnp.exp2((s - _lane_rep(m_new, tk)).astype(BF16))
        acc_scr[i, rows, :] = _lane_rep(alpha, 2 * PAIR) * acc_scr[i, rows, :] + _dg(p, v_aug, NN)
        m_scr[i, rows, :] = m_new


def _flash_finish(lam, sub_ref, scale, y_ref, acc_scr):
    for h in range(y_ref.shape[1] // PAIR):
        a1, a2 = acc_scr[2 * h], acc_scr[2 * h + 1]
        o = a1[:, :PAIR] / a1[:, PAIR:] - lam * (a2[:, :PAIR] / a2[:, PAIR:])
        ms = jnp.mean(o * o, axis=-1, keepdims=True)
        y = o * lax.rsqrt(ms + NORM_EPS) * sub_ref[...] * scale
        y_ref[:, h * PAIR:(h + 1) * PAIR] = y.astype(BF16)


def _attn_prompt_body(qt_ref, kt_ref, lam_ref, q_ref, k_ref, v_ref, sub_ref, y_ref, m_scr, acc_scr, *, tq, scale):
    s = pl.program_id(1)
    qi = qt_ref[s]
    kj = kt_ref[s]
    head = lambda h: slice(h * PAIR, (h + 1) * PAIR)

    @pl.when(kj == 0)
    def _():
        _flash_init(m_scr, acc_scr)

    nkt = k_ref.shape[0] // tq
    last_tile = kj * nkt + nkt - 1

    @pl.when(last_tile < qi)
    def _():
        _flash_update(q_ref[...], lambda h: k_ref[:, head(h)], lambda h: v_ref[:, head(h)], nkt * tq,
                      m_scr, acc_scr)

    for j in range(nkt):
        rows = slice(j * tq, (j + 1) * tq)
        k_of = lambda h, rows=rows: k_ref[rows, head(h)]
        v_of = lambda h, rows=rows: v_ref[rows, head(h)]
        tile = kj * nkt + j

        if j < nkt - 1:
            @pl.when((last_tile >= qi) & (tile < qi))
            def _(k_of=k_of, v_of=v_of):
                _flash_update(q_ref[...], k_of, v_of, tq, m_scr, acc_scr)

        @pl.when(tile == qi)
        def _(j=j):
            ng = 2 if tq % (2 * CHUNK) == 0 else 1
            grp = tq // ng
            for g in range(ng):
                nkeys = (g + 1) * grp
                keys = slice(j * tq, j * tq + nkeys)
                qrow = g * grp + lax.broadcasted_iota(jnp.int32, (grp, nkeys), 0)
                kcol = lax.broadcasted_iota(jnp.int32, (grp, nkeys), 1)
                mask = (kcol // CHUNK) <= (qrow // CHUNK)
                qrows = slice(g * grp, (g + 1) * grp)
                _flash_update(q_ref[qrows, :], lambda h, keys=keys: k_ref[keys, head(h)],
                              lambda h, keys=keys: v_ref[keys, head(h)], nkeys, m_scr, acc_scr, mask, qrows)
            _flash_finish(lam_ref[0], sub_ref, scale, y_ref, acc_scr)


def _attn_prompt(lam, q, kb, vb, subln, scale, batch, seq):
    n, dd = q.shape
    tq = min(512, seq)
    nq = seq // tq
    nh2 = 2 * dd // PAIR
    nkt = 2 if nq % 2 == 0 else 1
    pairs = [(i, j) for i in range(nq) for j in range(i // nkt + 1)]
    qt = jnp.array([p[0] for p in pairs], jnp.int32)
    kt = jnp.array([p[1] for p in pairs], jnp.int32)
    kv_spec = pl.BlockSpec((tq * nkt, dd), lambda b, s, qt, kt: (b * (nq // nkt) + kt[s], 0))
    q_spec = pl.BlockSpec((tq, dd), lambda b, s, qt, kt: (b * nq + qt[s], 0))
    return pl.pallas_call(
        functools.partial(_attn_prompt_body, tq=tq, scale=scale),
        grid_spec=pltpu.PrefetchScalarGridSpec(
            num_scalar_prefetch=2,
            grid=(batch, len(pairs)),
            in_specs=[pl.BlockSpec(memory_space=pltpu.SMEM), q_spec, kv_spec, kv_spec,
                      pl.BlockSpec((1, PAIR), lambda b, s, qt, kt: (0, 0))],
            out_specs=q_spec,
            scratch_shapes=[pltpu.VMEM((nh2, tq, PAIR), F32), pltpu.VMEM((nh2, tq, 2 * PAIR), F32)]),
        out_shape=jax.ShapeDtypeStruct((n, dd), BF16),
        compiler_params=_params("arbitrary", "arbitrary"),
        name="attn_prompt",
    )(qt, kt, lam, q, kb, vb, subln)


def _attn_sample_body(lam_ref, q_ref, ck_ref, cv_ref, k_ref, v_ref, sub_ref, y_ref, m_scr, acc_scr, *, scale):
    kj = pl.program_id(1)
    last = pl.num_programs(1) - 1

    @pl.when(kj == 0)
    def _():
        _flash_init(m_scr, acc_scr)

    head = lambda h: slice(h * PAIR, (h + 1) * PAIR)

    nh = q_ref.shape[1] // PAIR
    tk = ck_ref.shape[0] // nh

    @pl.when(kj < last)
    def _():
        _flash_update(q_ref[...], lambda h: ck_ref[pl.ds(h, tk, stride=nh), :].astype(BF16),
                      lambda h: cv_ref[pl.ds(h, tk, stride=nh), :].astype(BF16), tk, m_scr, acc_scr)

    @pl.when(kj == last)
    def _():
        _flash_update(q_ref[...], lambda h: k_ref[:, head(h)], lambda h: v_ref[:, head(h)],
                      k_ref.shape[0], m_scr, acc_scr)
        _flash_finish(lam_ref[0], sub_ref, scale, y_ref, acc_scr)


def _attn_sample(lam, q, cache_k, cache_v, layer, kb, vb, subln, scale, batch, seq):
    n, dd = q.shape
    past = cache_k.shape[2]
    tk = min(2048, past)
    nk = past // tk
    nh2 = 2 * dd // PAIR
    row = pl.BlockSpec((seq, dd), lambda b, j: (b, 0))
    nh = dd // PAIR
    cache_k = cache_k.reshape(cache_k.shape[:2] + (past * nh, PAIR))
    cache_v = cache_v.reshape(cache_v.shape[:2] + (past * nh, PAIR))
    cache = pl.BlockSpec((None, None, tk * nh, PAIR), lambda b, j: (layer, b, jnp.minimum(j, nk - 1), 0))
    return pl.pallas_call(
        functools.partial(_attn_sample_body, scale=scale),
        grid=(batch, nk + 1),
        in_specs=[pl.BlockSpec(memory_space=pltpu.SMEM), row, cache, cache, row, row, _const_spec((1, PAIR), 2)],
        out_specs=row,
        out_shape=jax.ShapeDtypeStruct((n, dd), BF16),
        scratch_shapes=[pltpu.VMEM((nh2, seq, PAIR), F32), pltpu.VMEM((nh2, seq, 2 * PAIR), F32)],
        compiler_params=_params("arbitrary", "arbitrary"),
        name="attn_sample",
    )(lam, q, cache_k, cache_v, kb, vb, subln)


def _ffn_body(h_ref, yr_ref, ya_ref, wo_ref, gf_ref, wup_ref, cw_ref, cb_ref, wdn_ref, cprev_ref,
              hout_ref, cnew_ref, carry_scr, *, dff, fc):
    t = pl.program_id(1)

    @pl.when(t == 0)
    def _():
        carry_scr[...] = cprev_ref[...]

    y = jnp.concatenate([yr_ref[...], ya_ref[...]], axis=1)
    h1 = h_ref[...] + _dg(y, wo_ref[...], NN)
    ms = jnp.mean(h1 * h1, axis=-1, keepdims=True)
    xn = (h1 * lax.rsqrt(ms + NORM_EPS) * gf_ref[...]).astype(BF16)
    tm = h1.shape[0]
    rowi = lax.broadcasted_iota(jnp.int32, (tm, fc), 0)
    acc = jnp.zeros(h1.shape, F32)

    def up(c):
        return (_dg(xn, wup_ref[:, c * fc:(c + 1) * fc], NN),
                _dg(xn, wup_ref[:, dff + c * fc:dff + (c + 1) * fc], NN))

    nxt = up(0)
    hs = []
    for c in range(dff // fc):
        sl = slice(c * fc, (c + 1) * fc)
        gt, u = nxt
        if c + 1 < dff // fc:
            nxt = up(c + 1)
        p1 = carry_scr[7:8, sl]
        p2 = carry_scr[6:7, sl]
        g1 = jnp.where(rowi == 0, p1, pltpu.roll(gt, 1, axis=0))
        g2 = jnp.where(rowi == 0, p2, jnp.where(rowi == 1, p1, pltpu.roll(gt, 2, axis=0)))
        gc = cb_ref[:, sl] + g2 * cw_ref[0:1, sl] + g1 * cw_ref[1:2, sl] + gt * cw_ref[2:3, sl]
        carry_scr[:, sl] = gt[tm - 8:, :]
        hs.append((gc * jax.nn.sigmoid(gc) * u).astype(BF16))
        if len(hs) == DOWN_GROUP or c + 1 == dff // fc:
            lo = (c + 1 - len(hs)) * fc
            acc = acc + _dg(jnp.concatenate(hs, axis=1), wdn_ref[lo:(c + 1) * fc, :], NN)
            hs = []
    hout_ref[...] = h1 + acc
    cnew_ref[...] = carry_scr[...]


def _ffn(h, yr, ya, wo, g_ffn, wup, cw, cb, wdn, conv_prev8, batch, seq):
    n, d = h.shape
    dff = cb.shape[-1]
    fc = 256
    tm = min(512, seq)
    nt = seq // tm
    row = lambda w: pl.BlockSpec((tm, w), lambda b, t: (b * nt + t, 0))
    st = pl.BlockSpec((None, 8, dff), lambda b, t: (b, 0, 0))
    cs = lambda a: _const_spec(a.shape, 2)
    return pl.pallas_call(
        functools.partial(_ffn_body, dff=dff, fc=fc),
        grid=(batch, nt),
        in_specs=[row(d), row(yr.shape[1]), row(ya.shape[1]), cs(wo), cs(g_ffn), cs(wup), cs(cw), cs(cb), cs(wdn), st],
        out_specs=[row(d), st],
        out_shape=[jax.ShapeDtypeStruct((n, d), F32), jax.ShapeDtypeStruct((batch, 8, dff), F32)],
        scratch_shapes=[pltpu.VMEM((8, dff), F32)],
        compiler_params=_params("arbitrary", "arbitrary"),
        name="ffn",
    )(h, yr, ya, wo, g_ffn, wup, cw, cb, wdn, conv_prev8)


DOWN_GROUP = 4
WKV_ROWS = 1024
WKV_SUB = 256


def _lambda_init(layer):
    return 0.8 - 0.6 * math.exp(-0.3 * layer)


def _run_group(x, depth, layers, ones_pair, shift_prev, wkv_prev, conv_prev, attend):
    batch, seq, d = x.shape
    h = x.reshape(batch * seq, d)
    ws, ss, cs = [], [], []
    nh = layers[0]["q_gain"].shape[-1] // PAIR
    k_all = jnp.zeros((depth, batch * seq * nh, PAIR), F32)
    v_all = jnp.zeros((depth, batch * seq * nh, PAIR), F32)
    for l in range(depth):
        lp = layers[l]
        ns = lp["mu"].shape[-1]
        dd = lp["q_gain"].shape[-1]
        dr = lp["w0"].shape[-1]
        dff = lp["conv_b"].shape[-1]
        ps, q, k_all, v_all, kb, vb = _inproj(h, lp["g_mix"], lp["w_in"], lp["q_gain"], lp["k_gain"],
                                              k_all, v_all, l, ns, dd)

        sp = jnp.zeros((batch, 1, ns), F32) if shift_prev is None else shift_prev[l][:, None, :]
        if wkv_prev is None:
            h0 = jnp.zeros((batch, HEAD, dr), F32)
        else:
            h0 = jnp.transpose(wkv_prev[l], (0, 3, 1, 2)).reshape(batch, HEAD, dr)
        yr, hout = _wkv(ps, sp, h0, lp, ones_pair, batch, seq)

        lam_init = _lambda_init(l)
        ya = attend(l, lp["lam"], q, kb, vb, lp["subln"], 1.0 - lam_init, batch, seq)

        cp = jnp.zeros((batch, 8, dff), F32) if conv_prev is None else jnp.pad(conv_prev[l], ((0, 0), (6, 0), (0, 0)))
        h, cnew = _ffn(h, yr, ya, lp["w_out"], lp["g_ffn"], lp["w_up"], lp["conv_w"], lp["conv_b"], lp["w_down"],
                       cp, batch, seq)

        ws.append(jnp.transpose(hout.reshape(batch, HEAD, dr // HEAD, HEAD), (0, 2, 3, 1)))
        ss.append(ps.reshape(batch, seq, ns)[:, -1])
        cs.append(cnew[:, 6:8])
    leaf = lambda t: t.reshape(depth, batch, seq, nh, PAIR)
    return (h.reshape(batch, seq, d), leaf(k_all), leaf(v_all), jnp.stack(ws), jnp.stack(ss), jnp.stack(cs))


def kernel(x_prompt, x_sample, cache_k, cache_v, state_wkv, state_shift, state_conv, g_mix, w_in, mu_shift, w0, w_decay, a0, w_aaa, w_gate, k_k, k_a, r_k, gn_w, gn_b, q_gain, k_gain, lambdas, subln_gain, w_out, g_ffn, w_ffn_in, conv_w, conv_b, w_ffn_out):
    depth = w_in.shape[0]
    dr = w0.shape[-1]
    ns = mu_shift.shape[-1]
    dd = (w_in.shape[-1] - ns) // 3
    assert dr % PAIR == 0 and dd % PAIR == 0 and q_gain.shape[-1] == HEAD and subln_gain.shape[-1] == PAIR
    assert x_prompt.shape[1] % CHUNK == 0 and x_sample.shape[1] == CHUNK

    row = lambda a: a.reshape(1, -1).astype(F32)
    layers = []
    for l in range(depth):
        lv = lambdas[l].astype(F32)
        lam = jnp.exp(jnp.sum(lv[0] * lv[1])) - jnp.exp(jnp.sum(lv[2] * lv[3])) + _lambda_init(l)
        layers.append(dict(
            g_mix=row(g_mix[l]), w_in=w_in[l].astype(BF16), mu=row(mu_shift[l]), w0=row(w0[l]), a0=row(a0[l]),
            w_decay=w_decay[l], w_aaa=w_aaa[l], w_gate=w_gate[l], k_k=row(k_k[l]), k_a=row(k_a[l]), r_k=row(r_k[l]),
            gn_w=row(gn_w[l]), gn_b=row(gn_b[l]),
            q_gain=row(jnp.tile(q_gain[l], dd // HEAD)), k_gain=row(jnp.tile(k_gain[l], dd // HEAD)),
            lam=lam.reshape(1), subln=row(subln_gain[l]), w_out=w_out[l].astype(BF16), g_ffn=row(g_ffn[l]),
            w_up=w_ffn_in[l].astype(BF16), conv_w=conv_w[l], conv_b=row(conv_b[l]), w_down=w_ffn_out[l].astype(BF16)))

    ones_pair = jnp.kron(jnp.eye(2, dtype=F32), jnp.ones((HEAD, HEAD), F32)).astype(BF16)

    def attend_prompt(l, lam, q, kb, vb, subln, scale, batch, seq):
        return _attn_prompt(lam, q, kb, vb, subln, scale, batch, seq)


    def attend_sample(l, lam, q, kb, vb, subln, scale, batch, seq):
        return _attn_sample(lam, q, cache_k, cache_v, l, kb, vb, subln, scale, batch, seq)

    yp, pk, pv, pw, ps_, pc = _run_group(x_prompt, depth, layers, ones_pair, None, None, None, attend_prompt)
    ys, sk, sv, sw, ss, sc = _run_group(x_sample, depth, layers, ones_pair, state_shift, state_wkv, state_conv,
                                        attend_sample)
    return (yp, ys, pk, pv, pw, ps_, pc, sk, sv, sw, ss, sc)
```

```python
import functools
import math

import jax
import jax.numpy as jnp
from jax import lax
from jax.experimental import pallas as pl
from jax.experimental.pallas import tpu as pltpu

F32 = jnp.float32
BF16 = jnp.bfloat16

HEAD = 64
PAIR = 2 * HEAD
CHUNK = 64
NORM_EPS = 1e-6
GN_EPS = 64e-5
NEG_BIG = -1e30
LOG2E = math.log2(math.e)

V7X_VMEM_BYTES = 64 * 1024 * 1024
VMEM_LIMIT = V7X_VMEM_BYTES * 7 // 8
ROW_TILE = 512
ATTN_TILE = 512
CACHE_TILE = 2048
FFN_COLS = 256
DOWN_GROUP = 4
WKV_ROWS = 1024
WKV_SUB = 256

NN = (((1,), (0,)), ((), ()))
NT = (((1,), (1,)), ((), ()))
TN = (((0,), (0,)), ((), ()))


def _dg(a, b, dn):
    return lax.dot_general(a, b, dn, preferred_element_type=F32)


def _split2(x):
    hi = x.astype(BF16)
    lo = (x - hi.astype(F32)).astype(BF16)
    return hi, lo


def _mm(a, b, dn=NN):
    return _dg(a.astype(BF16), b.astype(BF16), dn)


def _mm_exact_rhs(a, e):
    hi, lo = _split2(a)
    return _dg(hi, e, NN) + _dg(lo, e, NN)


def _mm_exact_lhs(e, b):
    hi, lo = _split2(b)
    return _dg(e, hi, NN) + _dg(e, lo, NN)


def _segsum(x, ones_bd):
    w = ones_bd.shape[0]
    parts = [_mm_exact_rhs(x[:, i:i + w], ones_bd) for i in range(0, x.shape[1], w)]
    return parts[0] if len(parts) == 1 else jnp.concatenate(parts, axis=1)


def _params(*sem):
    return pltpu.CompilerParams(dimension_semantics=sem, vmem_limit_bytes=VMEM_LIMIT)


def _const_spec(shape, grid_rank):
    zeros = (0,) * len(shape)
    if grid_rank == 1:
        return pl.BlockSpec(shape, lambda i: zeros, pipeline_mode=pl.Buffered(1))
    if grid_rank == 2:
        return pl.BlockSpec(shape, lambda i, j: zeros, pipeline_mode=pl.Buffered(1))
    return pl.BlockSpec(shape, lambda i, j, k: zeros, pipeline_mode=pl.Buffered(1))


def _inproj_body(x_ref, g_ref, w_ref, qg_ref, kg_ref, ones_ref, kall_ref, vall_ref,
                 ps_ref, q_ref, k_ref, v_ref, kb_ref, vb_ref, *, ns, dd):
    del kall_ref, vall_ref
    x = x_ref[...]
    ms = jnp.mean(x * x, axis=-1, keepdims=True)
    xn = (x * lax.rsqrt(ms + NORM_EPS) * g_ref[...]).astype(BF16)
    ps_ref[...] = _dg(xn, w_ref[:, :ns], NN)
    ones = ones_ref[...]

    def head_norm(t, gain):
        w = ones.shape[0]
        sq = (t * t).astype(BF16)
        ss = jnp.concatenate([_dg(sq[:, i:i + w], ones, NN) for i in range(0, dd, w)], axis=1)
        return t * lax.rsqrt(ss * (1.0 / HEAD) + NORM_EPS) * gain

    q = _dg(xn, w_ref[:, ns:ns + dd], NN)
    k = _dg(xn, w_ref[:, ns + dd:ns + 2 * dd], NN)
    v = _dg(xn, w_ref[:, ns + 2 * dd:ns + 3 * dd], NN)
    nh = dd // PAIR
    tm = x.shape[0]

    def put_heads(ref, t):
        for hd in range(nh):
            ref[pl.ds(hd, tm, stride=nh), :] = t[:, hd * PAIR:(hd + 1) * PAIR]

    put_heads(v_ref, v)
    vb_ref[...] = v.astype(BF16)
    q = head_norm(q, qg_ref[...])
    q_ref[...] = (q * (HEAD ** -0.5 * LOG2E)).astype(BF16)
    k = head_norm(k, kg_ref[...])
    put_heads(k_ref, k)
    kb_ref[...] = k.astype(BF16)


def _inproj(h, g, w_bf, q_gain, k_gain, k_all, v_all, layer, ns, dd):
    n, d = h.shape
    tm = min(ROW_TILE, n)
    nh = dd // PAIR
    leaf = pl.BlockSpec((None, tm * nh, PAIR), lambda i: (layer, i, 0))
    anywhere = pl.BlockSpec(memory_space=pl.ANY)
    ones = jnp.kron(jnp.eye(2 * PAIR // HEAD, dtype=F32), jnp.ones((HEAD, HEAD), F32)).astype(BF16)
    row = lambda w: pl.BlockSpec((tm, w), lambda i: (i, 0))
    return pl.pallas_call(
        functools.partial(_inproj_body, ns=ns, dd=dd),
        grid=(n // tm,),
        in_specs=[row(d), _const_spec((1, d), 1), _const_spec(w_bf.shape, 1),
                  _const_spec((1, dd), 1), _const_spec((1, dd), 1), _const_spec(ones.shape, 1),
                  anywhere, anywhere],
        out_specs=[row(ns), row(dd), leaf, leaf, row(dd), row(dd)],
        out_shape=[jax.ShapeDtypeStruct((n, ns), F32), jax.ShapeDtypeStruct((n, dd), BF16),
                   jax.ShapeDtypeStruct(k_all.shape, F32), jax.ShapeDtypeStruct(v_all.shape, F32),
                   jax.ShapeDtypeStruct((n, dd), BF16), jax.ShapeDtypeStruct((n, dd), BF16)],
        input_output_aliases={6: 2, 7: 3},
        compiler_params=_params("arbitrary"),
        name="inproj",
    )(h, g, w_bf, q_gain, k_gain, ones, k_all, v_all)


def _bd2(x, m0, swap=False):
    zero = jnp.zeros_like(x)
    first = jnp.where(m0, x, zero)
    second = jnp.where(m0, zero, x)
    return jnp.concatenate([second, first] if swap else [first, second], axis=0)


def _wkv_body(ps_ref, prev_ref, sp_ref, mu_ref, w0_ref, a0_ref, kk_ref, ka_ref, rk_ref,
              wd_ref, wa_ref, wg_ref, ones_ref, tril_ref, h0_ref, gnw_ref, gnb_ref,
              y_ref, hout_ref, h_scr, *, dr, nd, na, sub):
    c = pl.program_id(1)

    @pl.when(c == 0)
    def _():
        h_scr[...] = h0_ref[...]

    ns = ps_ref.shape[1]
    ones = ones_ref[...]
    sigmoid = lambda t: 0.5 + 0.5 * jnp.tanh(0.5 * t)
    seg = lambda t: jnp.concatenate([_dg(t[:, i:i + PAIR].astype(BF16), ones, NN) for i in range(0, dr, PAIR)],
                                    axis=1)

    lane = lax.broadcasted_iota(jnp.int32, (CHUNK, PAIR), 1)
    rr = lax.broadcasted_iota(jnp.int32, (CHUNK, PAIR), 0)
    m0 = lane < HEAD
    lane_in = jnp.where(m0, lane, lane - HEAD)
    strict = rr > lane_in
    incl = rr >= lane_in
    diag = rr == lane_in
    eye2 = jnp.where(diag, 1.0, 0.0).astype(F32)
    m0w = lax.broadcasted_iota(jnp.int32, (2 * CHUNK, PAIR), 1) < HEAD
    zero = jnp.zeros((CHUNK, PAIR), F32)
    zero2 = jnp.zeros((2 * CHUNK, PAIR), F32)

    bd = lambda t, swap=False: _bd2(t, m0, swap).astype(BF16)

    def prologue(i, pro):
        rows = slice(i * sub, (i + 1) * sub)
        x = ps_ref[rows, :]
        if i == 0:
            row0 = jnp.where(c == 0, sp_ref[...], prev_ref[7:8, :])
        else:
            row0 = ps_ref[i * sub - 1:i * sub, :]
        rowi = lax.broadcasted_iota(jnp.int32, (sub, ns), 0)
        prev = jnp.where(rowi == 0, row0, pltpu.roll(x, 1, axis=0))
        xs = x + (prev - x) * mu_ref[...]
        r = xs[:, :dr]
        k = xs[:, dr:2 * dr]
        v = xs[:, 2 * dr:3 * dr]
        o = 3 * dr
        t_w = jnp.tanh(xs[:, o:o + nd])
        xa = xs[:, o + nd:o + nd + na]
        s_g = sigmoid(xs[:, o + nd + na:])
        yield
        z = _mm(t_w, wd_ref[...])
        al = _mm(xa, wa_ref[...])
        gate = _mm(s_g, wg_ref[...])
        yield
        ld = -math.exp(-0.5) * sigmoid(w0_ref[...] + z)
        a = sigmoid(a0_ref[...] + al)
        kk = k * kk_ref[...]
        k2 = k * (1.0 + (a - 1.0) * ka_ref[...])
        kk_ss = seg(kk * kk)
        bonus = seg(r * k2 * rk_ref[...])
        cum = _mm_exact_lhs(tril_ref[...], ld)
        yield
        kk = kk * lax.rsqrt(jnp.maximum(kk_ss, 1e-24))
        bv = kk * a
        e_cum = jnp.exp(cum)
        e_neg = jnp.exp(-cum)
        chunk_start = lax.broadcasted_iota(jnp.int32, cum.shape, 0) % CHUNK == 0
        e_prev = jnp.where(chunk_start, 1.0, pltpu.roll(e_cum, 1, axis=0))
        pro.update(at=-kk * e_prev, rt=r * e_cum, bt=bv * e_neg, kt=k2 * e_neg, bv=bv, k2=k2, v=v,
                   e_cum=e_cum, e_neg=e_neg, gate=gate, bonus=bonus * v)

    def algebra(pro, res):
        chains = [(slice(ch * CHUNK, (ch + 1) * CHUNK), slice(j * PAIR, (j + 1) * PAIR), (ch + 1) * CHUNK - 1)
                  for ch in range(sub // CHUNK) for j in range(dr // PAIR)]
        atp = [pro["at"][rs, sl] for rs, sl, _ in chains]
        rtp = [pro["rt"][rs, sl] for rs, sl, _ in chains]
        vp = [pro["v"][rs, sl] for rs, sl, _ in chains]
        g0, g1 = [], []
        for n, (rs, sl, _) in enumerate(chains):
            btp, ktp = pro["bt"][rs, sl], pro["kt"][rs, sl]
            lhs = jnp.concatenate([atp[n], rtp[n]], axis=0)
            g0.append(_mm(jnp.where(m0w, lhs, zero2), jnp.concatenate([btp, ktp], axis=0), NT))
            g1.append(_mm(jnp.where(m0w, zero2, lhs), jnp.concatenate([ktp, btp], axis=0), NT))
        yield
        qp = [jnp.where(strict, jnp.where(m0, a[:CHUNK], b[:CHUNK]), zero) for a, b in zip(g0, g1)]
        akmk = [jnp.concatenate([jnp.where(strict, jnp.where(m0, b[:CHUNK], a[:CHUNK]), zero),
                                 jnp.where(incl, jnp.where(m0, b[CHUNK:], a[CHUNK:]), zero)], axis=0)
                for a, b in zip(g0, g1)]
        mb = [jnp.where(incl, jnp.where(m0, a[CHUNK:], b[CHUNK:]), zero) for a, b in zip(g0, g1)]
        kv = [_mm(l, bd(t, True)) for l, t in zip(akmk, vp)]
        yield
        tm = [eye2 + t for t in qp]
        bq = [bd(t) for t in qp]
        for _ in range(5):
            qp = [_mm(t, b) for t, b in zip(qp, bq)]
            yield
            bq = [bd(t) for t in qp]
            tm = [t + _mm(t, b) for t, b in zip(tm, bq)]
            yield
        au = [_mm(t, jnp.concatenate([bd(a), bd(k[:CHUNK])], axis=1)) for t, a, k in zip(tm, atp, kv)]
        yield
        ry = [_mm(t, jnp.concatenate([bd(a[:, :PAIR]), bd(a[:, PAIR:])], axis=1)) for t, a in zip(mb, au)]
        ph = []
        for n, (rs, sl, last) in enumerate(chains):
            e_end = pro["e_cum"][last:last + 1, sl] * pro["e_neg"][rs, sl]
            rhs = jnp.concatenate([au[n], jnp.concatenate([zero, vp[n]], axis=1)], axis=0)
            ph.append(_mm(jnp.concatenate([pro["bv"][rs, sl] * e_end, pro["k2"][rs, sl] * e_end], axis=0), rhs, TN))
        yield
        res["rp"] = [(rtp[n] + ry[n][:, :PAIR]).astype(BF16) for n in range(len(chains))]
        res["y0"] = [kv[n][CHUNK:] + ry[n][:, PAIR:] for n in range(len(chains))]
        res["pc"] = [(jnp.where(m0, ph[n][:CHUNK, :PAIR], ph[n][CHUNK:, :PAIR])
                      + jnp.where(diag, pro["e_cum"][last:last + 1, sl], 0.0)).astype(BF16)
                     for n, (rs, sl, last) in enumerate(chains)]
        res["hinc"] = [jnp.where(m0, ph[n][:CHUNK, PAIR:], ph[n][CHUNK:, PAIR:]) for n in range(len(chains))]

    npair = dr // PAIR

    def scan(i, pro, res, hs):
        ys = []
        for ch in range(sub // CHUNK):
            for j in range(npair):
                n = ch * npair + j
                out = _mm(jnp.concatenate([res["rp"][n], res["pc"][n]], axis=0), _bd2(hs[j], m0))
                ys.append(res["y0"][n] + out[:CHUNK])
                hs[j] = res["hinc"][n] + out[CHUNK:]
            yield
        y = jnp.concatenate([jnp.concatenate(ys[ch * npair:(ch + 1) * npair], axis=1)
                             for ch in range(sub // CHUNK)], axis=0)
        mean = _segsum(y, ones) * (1.0 / HEAD)
        d = y - mean
        var = _segsum(d * d, ones) * (1.0 / HEAD)
        yn = d * lax.rsqrt(var + GN_EPS) * gnw_ref[...] + gnb_ref[...]
        y_ref[i * sub:(i + 1) * sub, :] = ((yn + pro["bonus"]) * pro["gate"]).astype(BF16)

    nsub = ps_ref.shape[0] // sub
    pros = [dict() for _ in range(nsub)]
    hs = [h_scr[:, j * PAIR:(j + 1) * PAIR] for j in range(npair)]
    for _ in prologue(0, pros[0]):
        pass
    behind = iter(())
    for i in range(nsub):
        ahead = prologue(i + 1, pros[i + 1]) if i + 1 < nsub else iter(())
        res = {}
        for stage, _ in enumerate(algebra(pros[i], res)):
            next(ahead if stage % 2 == 0 else behind, None)
        for _ in ahead:
            pass
        for _ in behind:
            pass
        behind = scan(i, pros[i], res, hs)
    for _ in behind:
        pass
    for j in range(npair):
        h_scr[:, j * PAIR:(j + 1) * PAIR] = hs[j]
    hout_ref[...] = h_scr[...]


def _wkv(ps, shift_prev, h0, lp, ones_pair, batch, seq):
    n, ns = ps.shape
    dr = lp["w0"].shape[-1]
    nd = lp["w_decay"].shape[0]
    na = lp["w_aaa"].shape[0]
    rows = min(WKV_ROWS, seq)
    sub = min(WKV_SUB, rows)
    nc = seq // rows
    blk = rows // 8
    tril = jnp.kron(jnp.eye(sub // CHUNK, dtype=F32), jnp.tril(jnp.ones((CHUNK, CHUNK), F32))).astype(BF16)
    row = lambda w: pl.BlockSpec((rows, w), lambda b, c: (b * nc + c, 0))
    st = pl.BlockSpec((None, HEAD, dr), lambda b, c: (b, 0, 0))
    cs = lambda a: _const_spec(a.shape, 2)
    consts = [lp["mu"], lp["w0"], lp["a0"], lp["k_k"], lp["k_a"], lp["r_k"],
              lp["w_decay"], lp["w_aaa"], lp["w_gate"], ones_pair, tril]
    return pl.pallas_call(
        functools.partial(_wkv_body, dr=dr, nd=nd, na=na, sub=sub),
        grid=(batch, nc),
        in_specs=[row(ns),
                  pl.BlockSpec((8, ns), lambda b, c: (jnp.maximum((b * nc + c) * blk - 1, 0), 0)),
                  pl.BlockSpec((None, 1, ns), lambda b, c: (b, 0, 0))] + [cs(a) for a in consts]
                 + [st, cs(lp["gn_w"]), cs(lp["gn_b"])],
        out_specs=[row(dr), st],
        out_shape=[jax.ShapeDtypeStruct((n, dr), BF16), jax.ShapeDtypeStruct((batch, HEAD, dr), F32)],
        scratch_shapes=[pltpu.VMEM((HEAD, dr), F32)],
        compiler_params=_params("arbitrary", "arbitrary"),
        name="wkv",
    )(ps, ps, shift_prev, *consts, h0, lp["gn_w"], lp["gn_b"])


def _flash_init(m_scr, acc_scr):
    m_scr[...] = jnp.full(m_scr.shape, NEG_BIG, F32)
    acc_scr[...] = jnp.zeros(acc_scr.shape, F32)


def _lane_rep(x, width):
    if width % PAIR == 0:
        return x if width == PAIR else jnp.concatenate([x] * (width // PAIR), axis=1)
    return x[:, :width]


def _flash_update(q, k_of, v_of, tk, m_scr, acc_scr, mask=None, rows=slice(None)):
    tq = q.shape[0]
    first = lax.broadcasted_iota(jnp.int32, (tq, PAIR), 1) < HEAD
    zq = jnp.zeros((tq, PAIR), BF16)
    ones = jnp.ones((tk, PAIR), BF16)
    n_maps = 2 * (q.shape[1] // PAIR)

    def scores(i):
        qh = q[:, (i // 2) * PAIR:(i // 2 + 1) * PAIR]
        qm = jnp.where(first, qh, zq) if i % 2 == 0 else jnp.where(first, zq, qh)
        return _dg(qm, k_of(i // 2), NT)

    s_next = scores(0)
    for i in range(n_maps):
        s = s_next
        if i + 1 < n_maps:
            s_next = scores(i + 1)
        if mask is not None:
            s = jnp.where(mask, s, NEG_BIG)
        v_aug = jnp.concatenate([v_of(i // 2), ones], axis=1)
        m_prev = m_scr[i, rows, :]
        m_new = jnp.maximum(m_prev, jnp.max(s, axis=1, keepdims=True))
        alpha = jnp.exp2(m_prev - m_new)
        p = jnp.exp2((s - _lane_rep(m_new, tk)).astype(BF16))
        acc_scr[i, rows, :] = _lane_rep(alpha, 2 * PAIR) * acc_scr[i, rows, :] + _dg(p, v_aug, NN)
        m_scr[i, rows, :] = m_new


def _flash_finish(lam, sub_ref, scale, y_ref, acc_scr):
    for h in range(y_ref.shape[1] // PAIR):
        a1, a2 = acc_scr[2 * h], acc_scr[2 * h + 1]
        o = a1[:, :PAIR] / a1[:, PAIR:] - lam * (a2[:, :PAIR] / a2[:, PAIR:])
        ms = jnp.mean(o * o, axis=-1, keepdims=True)
        y = o * lax.rsqrt(ms + NORM_EPS) * sub_ref[...] * scale
        y_ref[:, h * PAIR:(h + 1) * PAIR] = y.astype(BF16)


def _attn_prompt_body(qt_ref, kt_ref, lam_ref, q_ref, k_ref, v_ref, sub_ref, y_ref, m_scr, acc_scr, *, tq, scale):
    s = pl.program_id(1)
    qi = qt_ref[s]
    kj = kt_ref[s]
    head = lambda h: slice(h * PAIR, (h + 1) * PAIR)

    @pl.when(kj == 0)
    def _():
        _flash_init(m_scr, acc_scr)

    nkt = k_ref.shape[0] // tq
    last_tile = kj * nkt + nkt - 1

    @pl.when(last_tile < qi)
    def _():
        _flash_update(q_ref[...], lambda h: k_ref[:, head(h)], lambda h: v_ref[:, head(h)], nkt * tq,
                      m_scr, acc_scr)

    for j in range(nkt):
        rows = slice(j * tq, (j + 1) * tq)
        k_of = lambda h, rows=rows: k_ref[rows, head(h)]
        v_of = lambda h, rows=rows: v_ref[rows, head(h)]
        tile = kj * nkt + j

        if j < nkt - 1:
            @pl.when((last_tile >= qi) & (tile < qi))
            def _(k_of=k_of, v_of=v_of):
                _flash_update(q_ref[...], k_of, v_of, tq, m_scr, acc_scr)

        @pl.when(tile == qi)
        def _(j=j):
            ng = 2 if tq % (2 * CHUNK) == 0 else 1
            grp = tq // ng
            for g in range(ng):
                nkeys = (g + 1) * grp
                keys = slice(j * tq, j * tq + nkeys)
                qrow = g * grp + lax.broadcasted_iota(jnp.int32, (grp, nkeys), 0)
                kcol = lax.broadcasted_iota(jnp.int32, (grp, nkeys), 1)
                mask = (kcol // CHUNK) <= (qrow // CHUNK)
                qrows = slice(g * grp, (g + 1) * grp)
                _flash_update(q_ref[qrows, :], lambda h, keys=keys: k_ref[keys, head(h)],
                              lambda h, keys=keys: v_ref[keys, head(h)], nkeys, m_scr, acc_scr, mask, qrows)
            _flash_finish(lam_ref[0], sub_ref, scale, y_ref, acc_scr)


def _attn_prompt(lam, q, kb, vb, subln, scale, batch, seq):
    n, dd = q.shape
    tq = min(ATTN_TILE, seq)
    nq = seq // tq
    nh2 = 2 * dd // PAIR
    nkt = 2 if nq % 2 == 0 else 1
    pairs = [(i, j) for i in range(nq) for j in range(i // nkt + 1)]
    qt = jnp.array([p[0] for p in pairs], jnp.int32)
    kt = jnp.array([p[1] for p in pairs], jnp.int32)
    kv_spec = pl.BlockSpec((tq * nkt, dd), lambda b, s, qt, kt: (b * (nq // nkt) + kt[s], 0))
    q_spec = pl.BlockSpec((tq, dd), lambda b, s, qt, kt: (b * nq + qt[s], 0))
    return pl.pallas_call(
        functools.partial(_attn_prompt_body, tq=tq, scale=scale),
        grid_spec=pltpu.PrefetchScalarGridSpec(
            num_scalar_prefetch=2,
            grid=(batch, len(pairs)),
            in_specs=[pl.BlockSpec(memory_space=pltpu.SMEM), q_spec, kv_spec, kv_spec,
                      pl.BlockSpec((1, PAIR), lambda b, s, qt, kt: (0, 0))],
            out_specs=q_spec,
            scratch_shapes=[pltpu.VMEM((nh2, tq, PAIR), F32), pltpu.VMEM((nh2, tq, 2 * PAIR), F32)]),
        out_shape=jax.ShapeDtypeStruct((n, dd), BF16),
        compiler_params=_params("arbitrary", "arbitrary"),
        name="attn_prompt",
    )(qt, kt, lam, q, kb, vb, subln)


def _attn_sample_body(lam_ref, q_ref, ck_ref, cv_ref, k_ref, v_ref, sub_ref, y_ref, m_scr, acc_scr, *, scale):
    kj = pl.program_id(1)
    last = pl.num_programs(1) - 1
    head = lambda h: slice(h * PAIR, (h + 1) * PAIR)
    nh = q_ref.shape[1] // PAIR
    tk = ck_ref.shape[0] // nh

    @pl.when(kj == 0)
    def _():
        _flash_init(m_scr, acc_scr)

    @pl.when(kj < last)
    def _():
        _flash_update(q_ref[...], lambda h: ck_ref[pl.ds(h, tk, stride=nh), :].astype(BF16),
                      lambda h: cv_ref[pl.ds(h, tk, stride=nh), :].astype(BF16), tk, m_scr, acc_scr)

    @pl.when(kj == last)
    def _():
        _flash_update(q_ref[...], lambda h: k_ref[:, head(h)], lambda h: v_ref[:, head(h)],
                      k_ref.shape[0], m_scr, acc_scr)
        _flash_finish(lam_ref[0], sub_ref, scale, y_ref, acc_scr)


def _attn_sample(lam, q, cache_k, cache_v, layer, kb, vb, subln, scale, batch, seq):
    n, dd = q.shape
    past = cache_k.shape[2]
    tk = min(CACHE_TILE, past)
    nk = past // tk
    nh2 = 2 * dd // PAIR
    row = pl.BlockSpec((seq, dd), lambda b, j: (b, 0))
    nh = dd // PAIR
    cache_k = cache_k.reshape(cache_k.shape[:2] + (past * nh, PAIR))
    cache_v = cache_v.reshape(cache_v.shape[:2] + (past * nh, PAIR))
    cache = pl.BlockSpec((None, None, tk * nh, PAIR), lambda b, j: (layer, b, jnp.minimum(j, nk - 1), 0))
    return pl.pallas_call(
        functools.partial(_attn_sample_body, scale=scale),
        grid=(batch, nk + 1),
        in_specs=[pl.BlockSpec(memory_space=pltpu.SMEM), row, cache, cache, row, row, _const_spec((1, PAIR), 2)],
        out_specs=row,
        out_shape=jax.ShapeDtypeStruct((n, dd), BF16),
        scratch_shapes=[pltpu.VMEM((nh2, seq, PAIR), F32), pltpu.VMEM((nh2, seq, 2 * PAIR), F32)],
        compiler_params=_params("arbitrary", "arbitrary"),
        name="attn_sample",
    )(lam, q, cache_k, cache_v, kb, vb, subln)


def _ffn_body(h_ref, yr_ref, ya_ref, wo_ref, gf_ref, wup_ref, cw_ref, cb_ref, wdn_ref, cprev_ref,
              hout_ref, cnew_ref, carry_scr, *, dff, fc):
    t = pl.program_id(1)

    @pl.when(t == 0)
    def _():
        carry_scr[...] = cprev_ref[...]

    y = jnp.concatenate([yr_ref[...], ya_ref[...]], axis=1)
    h1 = h_ref[...] + _dg(y, wo_ref[...], NN)
    ms = jnp.mean(h1 * h1, axis=-1, keepdims=True)
    xn = (h1 * lax.rsqrt(ms + NORM_EPS) * gf_ref[...]).astype(BF16)
    tm = h1.shape[0]
    rowi = lax.broadcasted_iota(jnp.int32, (tm, fc), 0)
    acc = jnp.zeros(h1.shape, F32)

    def up(c):
        return (_dg(xn, wup_ref[:, c * fc:(c + 1) * fc], NN),
                _dg(xn, wup_ref[:, dff + c * fc:dff + (c + 1) * fc], NN))

    nxt = up(0)
    hs = []
    for c in range(dff // fc):
        sl = slice(c * fc, (c + 1) * fc)
        gt, u = nxt
        if c + 1 < dff // fc:
            nxt = up(c + 1)
        p1 = carry_scr[7:8, sl]
        p2 = carry_scr[6:7, sl]
        g1 = jnp.where(rowi == 0, p1, pltpu.roll(gt, 1, axis=0))
        g2 = jnp.where(rowi == 0, p2, jnp.where(rowi == 1, p1, pltpu.roll(gt, 2, axis=0)))
        gc = cb_ref[:, sl] + g2 * cw_ref[0:1, sl] + g1 * cw_ref[1:2, sl] + gt * cw_ref[2:3, sl]
        carry_scr[:, sl] = gt[tm - 8:, :]
        hs.append((gc * jax.nn.sigmoid(gc) * u).astype(BF16))
        if len(hs) == DOWN_GROUP or c + 1 == dff // fc:
            lo = (c + 1 - len(hs)) * fc
            acc = acc + _dg(jnp.concatenate(hs, axis=1), wdn_ref[lo:(c + 1) * fc, :], NN)
            hs = []
    hout_ref[...] = h1 + acc
    cnew_ref[...] = carry_scr[...]


def _ffn(h, yr, ya, wo, g_ffn, wup, cw, cb, wdn, conv_prev8, batch, seq):
    n, d = h.shape
    dff = cb.shape[-1]
    fc = FFN_COLS
    tm = min(ROW_TILE, seq)
    nt = seq // tm
    row = lambda w: pl.BlockSpec((tm, w), lambda b, t: (b * nt + t, 0))
    st = pl.BlockSpec((None, 8, dff), lambda b, t: (b, 0, 0))
    cs = lambda a: _const_spec(a.shape, 2)
    return pl.pallas_call(
        functools.partial(_ffn_body, dff=dff, fc=fc),
        grid=(batch, nt),
        in_specs=[row(d), row(yr.shape[1]), row(ya.shape[1]), cs(wo), cs(g_ffn), cs(wup), cs(cw), cs(cb), cs(wdn), st],
        out_specs=[row(d), st],
        out_shape=[jax.ShapeDtypeStruct((n, d), F32), jax.ShapeDtypeStruct((batch, 8, dff), F32)],
        scratch_shapes=[pltpu.VMEM((8, dff), F32)],
        compiler_params=_params("arbitrary", "arbitrary"),
        name="ffn",
    )(h, yr, ya, wo, g_ffn, wup, cw, cb, wdn, conv_prev8)


def _lambda_init(layer):
    return 0.8 - 0.6 * math.exp(-0.3 * layer)


def _run_group(x, depth, layers, ones_pair, shift_prev, wkv_prev, conv_prev, attend):
    batch, seq, d = x.shape
    h = x.reshape(batch * seq, d)
    ws, ss, cs = [], [], []
    nh = layers[0]["q_gain"].shape[-1] // PAIR
    k_all = jnp.zeros((depth, batch * seq * nh, PAIR), F32)
    v_all = jnp.zeros((depth, batch * seq * nh, PAIR), F32)
    for l in range(depth):
        lp = layers[l]
        ns = lp["mu"].shape[-1]
        dd = lp["q_gain"].shape[-1]
        dr = lp["w0"].shape[-1]
        dff = lp["conv_b"].shape[-1]
        ps, q, k_all, v_all, kb, vb = _inproj(h, lp["g_mix"], lp["w_in"], lp["q_gain"], lp["k_gain"],
                                              k_all, v_all, l, ns, dd)

        sp = jnp.zeros((batch, 1, ns), F32) if shift_prev is None else shift_prev[l][:, None, :]
        if wkv_prev is None:
            h0 = jnp.zeros((batch, HEAD, dr), F32)
        else:
            h0 = jnp.transpose(wkv_prev[l], (0, 3, 1, 2)).reshape(batch, HEAD, dr)
        yr, hout = _wkv(ps, sp, h0, lp, ones_pair, batch, seq)

        lam_init = _lambda_init(l)
        ya = attend(l, lp["lam"], q, kb, vb, lp["subln"], 1.0 - lam_init, batch, seq)

        cp = jnp.zeros((batch, 8, dff), F32) if conv_prev is None else jnp.pad(conv_prev[l], ((0, 0), (6, 0), (0, 0)))
        h, cnew = _ffn(h, yr, ya, lp["w_out"], lp["g_ffn"], lp["w_up"], lp["conv_w"], lp["conv_b"], lp["w_down"],
                       cp, batch, seq)

        ws.append(jnp.transpose(hout.reshape(batch, HEAD, dr // HEAD, HEAD), (0, 2, 3, 1)))
        ss.append(ps.reshape(batch, seq, ns)[:, -1])
        cs.append(cnew[:, 6:8])
    leaf = lambda t: t.reshape(depth, batch, seq, nh, PAIR)
    return (h.reshape(batch, seq, d), leaf(k_all), leaf(v_all), jnp.stack(ws), jnp.stack(ss), jnp.stack(cs))


def kernel(x_prompt, x_sample, cache_k, cache_v, state_wkv, state_shift, state_conv, g_mix, w_in, mu_shift, w0, w_decay, a0, w_aaa, w_gate, k_k, k_a, r_k, gn_w, gn_b, q_gain, k_gain, lambdas, subln_gain, w_out, g_ffn, w_ffn_in, conv_w, conv_b, w_ffn_out):
    depth = w_in.shape[0]
    dr = w0.shape[-1]
    ns = mu_shift.shape[-1]
    dd = (w_in.shape[-1] - ns) // 3
    assert dr % PAIR == 0 and dd % PAIR == 0 and q_gain.shape[-1] == HEAD and subln_gain.shape[-1] == PAIR
    assert x_prompt.shape[1] % CHUNK == 0 and x_sample.shape[1] == CHUNK

    row = lambda a: a.reshape(1, -1).astype(F32)
    layers = []
    for l in range(depth):
        lv = lambdas[l].astype(F32)
        lam = jnp.exp(jnp.sum(lv[0] * lv[1])) - jnp.exp(jnp.sum(lv[2] * lv[3])) + _lambda_init(l)
        layers.append(dict(
            g_mix=row(g_mix[l]), w_in=w_in[l].astype(BF16), mu=row(mu_shift[l]), w0=row(w0[l]), a0=row(a0[l]),
            w_decay=w_decay[l], w_aaa=w_aaa[l], w_gate=w_gate[l], k_k=row(k_k[l]), k_a=row(k_a[l]), r_k=row(r_k[l]),
            gn_w=row(gn_w[l]), gn_b=row(gn_b[l]),
            q_gain=row(jnp.tile(q_gain[l], dd // HEAD)), k_gain=row(jnp.tile(k_gain[l], dd // HEAD)),
            lam=lam.reshape(1), subln=row(subln_gain[l]), w_out=w_out[l].astype(BF16), g_ffn=row(g_ffn[l]),
            w_up=w_ffn_in[l].astype(BF16), conv_w=conv_w[l], conv_b=row(conv_b[l]), w_down=w_ffn_out[l].astype(BF16)))

    ones_pair = jnp.kron(jnp.eye(2, dtype=F32), jnp.ones((HEAD, HEAD), F32)).astype(BF16)

    def attend_prompt(l, lam, q, kb, vb, subln, scale, batch, seq):
        return _attn_prompt(lam, q, kb, vb, subln, scale, batch, seq)

    def attend_sample(l, lam, q, kb, vb, subln, scale, batch, seq):
        return _attn_sample(lam, q, cache_k, cache_v, l, kb, vb, subln, scale, batch, seq)

    yp, pk, pv, pw, ps_, pc = _run_group(x_prompt, depth, layers, ones_pair, None, None, None, attend_prompt)
    ys, sk, sv, sw, ss, sc = _run_group(x_sample, depth, layers, ones_pair, state_shift, state_wkv, state_conv,
                                        attend_sample)
    return (yp, ys, pk, pv, pw, ps_, pc, sk, sv, sw, ss, sc)
```

```python
import functools
import math

import jax
import jax.numpy as jnp
from jax import lax
from jax.experimental import pallas as pl
from jax.experimental.pallas import tpu as pltpu

F32 = jnp.float32
BF16 = jnp.bfloat16

HEAD = 64
PAIR = 2 * HEAD
CHUNK = 64
NORM_EPS = 1e-6
GN_EPS = 64e-5
NEG_BIG = -1e30
LOG2E = math.log2(math.e)

V7X_VMEM_BYTES = 64 * 1024 * 1024
VMEM_LIMIT = V7X_VMEM_BYTES * 7 // 8
ROW_TILE = 512
ATTN_TILE = 512
CACHE_TILE = 2048
FFN_COLS = 256
DOWN_GROUP = 4
WKV_ROWS = 1024
WKV_SUB = 256

NN = (((1,), (0,)), ((), ()))
NT = (((1,), (1,)), ((), ()))
TN = (((0,), (0,)), ((), ()))


def _dg(a, b, dn):
    return lax.dot_general(a, b, dn, preferred_element_type=F32)


def _split2(x):
    hi = x.astype(BF16)
    lo = (x - hi.astype(F32)).astype(BF16)
    return hi, lo


def _mm(a, b, dn=NN):
    return _dg(a.astype(BF16), b.astype(BF16), dn)


def _mm_exact_rhs(a, e):
    hi, lo = _split2(a)
    return _dg(hi, e, NN) + _dg(lo, e, NN)


def _mm_exact_lhs(e, b):
    hi, lo = _split2(b)
    return _dg(e, hi, NN) + _dg(e, lo, NN)


def _segsum(x, ones_bd):
    w = ones_bd.shape[0]
    parts = [_mm_exact_rhs(x[:, i:i + w], ones_bd) for i in range(0, x.shape[1], w)]
    return parts[0] if len(parts) == 1 else jnp.concatenate(parts, axis=1)


def _params(*sem):
    return pltpu.CompilerParams(dimension_semantics=sem, vmem_limit_bytes=VMEM_LIMIT)


def _const_spec(shape, grid_rank):
    zeros = (0,) * len(shape)
    if grid_rank == 1:
        return pl.BlockSpec(shape, lambda i: zeros, pipeline_mode=pl.Buffered(1))
    if grid_rank == 2:
        return pl.BlockSpec(shape, lambda i, j: zeros, pipeline_mode=pl.Buffered(1))
    return pl.BlockSpec(shape, lambda i, j, k: zeros, pipeline_mode=pl.Buffered(1))


def _inproj_body(x_ref, g_ref, w_ref, qg_ref, kg_ref, ones_ref, *refs, ns, dd):
    ps_ref, q_ref, k_ref, v_ref, kb_ref, vb_ref = refs[-6:]
    x = x_ref[...]
    ms = jnp.mean(x * x, axis=-1, keepdims=True)
    xn = (x * lax.rsqrt(ms + NORM_EPS) * g_ref[...]).astype(BF16)
    ps_ref[...] = _dg(xn, w_ref[:, :ns], NN)
    ones = ones_ref[...]

    def head_norm(t, gain):
        w = ones.shape[0]
        sq = (t * t).astype(BF16)
        ss = jnp.concatenate([_dg(sq[:, i:i + w], ones, NN) for i in range(0, dd, w)], axis=1)
        return t * lax.rsqrt(ss * (1.0 / HEAD) + NORM_EPS) * gain

    q = _dg(xn, w_ref[:, ns:ns + dd], NN)
    k = _dg(xn, w_ref[:, ns + dd:ns + 2 * dd], NN)
    v = _dg(xn, w_ref[:, ns + 2 * dd:ns + 3 * dd], NN)
    nh = dd // PAIR
    tm = x.shape[0]

    def put_heads(ref, t):
        for hd in range(nh):
            ref[pl.ds(hd, tm, stride=nh), :] = t[:, hd * PAIR:(hd + 1) * PAIR]

    put_heads(v_ref, v)
    vb_ref[...] = v.astype(BF16)
    q = head_norm(q, qg_ref[...])
    q_ref[...] = (q * (HEAD ** -0.5 * LOG2E)).astype(BF16)
    k = head_norm(k, kg_ref[...])
    put_heads(k_ref, k)
    kb_ref[...] = k.astype(BF16)


def _inproj(h, g, w_bf, q_gain, k_gain, leaves, layer, depth, ns, dd):
    n, d = h.shape
    tm = min(ROW_TILE, n)
    nh = dd // PAIR
    leaf = pl.BlockSpec((None, tm * nh, PAIR), lambda i: (layer, i, 0))
    leaf_shape = jax.ShapeDtypeStruct((depth, n * nh, PAIR), F32)
    extra = [] if leaves is None else list(leaves)
    ones = jnp.kron(jnp.eye(2 * PAIR // HEAD, dtype=F32), jnp.ones((HEAD, HEAD), F32)).astype(BF16)
    row = lambda w: pl.BlockSpec((tm, w), lambda i: (i, 0))
    return pl.pallas_call(
        functools.partial(_inproj_body, ns=ns, dd=dd),
        grid=(n // tm,),
        in_specs=[row(d), _const_spec((1, d), 1), _const_spec(w_bf.shape, 1),
                  _const_spec((1, dd), 1), _const_spec((1, dd), 1), _const_spec(ones.shape, 1)]
                 + [pl.BlockSpec(memory_space=pl.ANY)] * len(extra),
        out_specs=[row(ns), row(dd), leaf, leaf, row(dd), row(dd)],
        out_shape=[jax.ShapeDtypeStruct((n, ns), F32), jax.ShapeDtypeStruct((n, dd), BF16),
                   leaf_shape, leaf_shape,
                   jax.ShapeDtypeStruct((n, dd), BF16), jax.ShapeDtypeStruct((n, dd), BF16)],
        input_output_aliases={6: 2, 7: 3} if extra else {},
        compiler_params=_params("arbitrary"),
        name="inproj",
    )(h, g, w_bf, q_gain, k_gain, ones, *extra)


def _bd2(x, m0, swap=False):
    zero = jnp.zeros_like(x)
    first = jnp.where(m0, x, zero)
    second = jnp.where(m0, zero, x)
    return jnp.concatenate([second, first] if swap else [first, second], axis=0)


def _wkv_body(ps_ref, prev_ref, sp_ref, mu_ref, w0_ref, a0_ref, kk_ref, ka_ref, rk_ref,
              wd_ref, wa_ref, wg_ref, ones_ref, tril_ref, h0_ref, gnw_ref, gnb_ref,
              y_ref, hout_ref, h_scr, *, dr, nd, na, sub):
    c = pl.program_id(1)

    @pl.when(c == 0)
    def _():
        h_scr[...] = h0_ref[...]

    ns = ps_ref.shape[1]
    ones = ones_ref[...]
    sigmoid = lambda t: 0.5 + 0.5 * jnp.tanh(0.5 * t)
    seg = lambda t: jnp.concatenate([_dg(t[:, i:i + PAIR].astype(BF16), ones, NN) for i in range(0, dr, PAIR)],
                                    axis=1)

    lane = lax.broadcasted_iota(jnp.int32, (CHUNK, PAIR), 1)
    rr = lax.broadcasted_iota(jnp.int32, (CHUNK, PAIR), 0)
    m0 = lane < HEAD
    lane_in = jnp.where(m0, lane, lane - HEAD)
    strict = rr > lane_in
    incl = rr >= lane_in
    diag = rr == lane_in
    eye2 = jnp.where(diag, 1.0, 0.0).astype(F32)
    m0w = lax.broadcasted_iota(jnp.int32, (2 * CHUNK, PAIR), 1) < HEAD
    zero = jnp.zeros((CHUNK, PAIR), F32)
    zero2 = jnp.zeros((2 * CHUNK, PAIR), F32)

    bd = lambda t, swap=False: _bd2(t, m0, swap).astype(BF16)

    def prologue(i, pro):
        rows = slice(i * sub, (i + 1) * sub)
        x = ps_ref[rows, :]
        if i == 0:
            row0 = jnp.where(c == 0, sp_ref[...], prev_ref[7:8, :])
        else:
            row0 = ps_ref[i * sub - 1:i * sub, :]
        rowi = lax.broadcasted_iota(jnp.int32, (sub, ns), 0)
        prev = jnp.where(rowi == 0, row0, pltpu.roll(x, 1, axis=0))
        xs = x + (prev - x) * mu_ref[...]
        r = xs[:, :dr]
        k = xs[:, dr:2 * dr]
        v = xs[:, 2 * dr:3 * dr]
        o = 3 * dr
        t_w = jnp.tanh(xs[:, o:o + nd])
        xa = xs[:, o + nd:o + nd + na]
        s_g = sigmoid(xs[:, o + nd + na:])
        yield
        z = _mm(t_w, wd_ref[...])
        al = _mm(xa, wa_ref[...])
        gate = _mm(s_g, wg_ref[...])
        yield
        ld = -math.exp(-0.5) * sigmoid(w0_ref[...] + z)
        a = sigmoid(a0_ref[...] + al)
        kk = k * kk_ref[...]
        k2 = k * (1.0 + (a - 1.0) * ka_ref[...])
        kk_ss = seg(kk * kk)
        bonus = seg(r * k2 * rk_ref[...])
        cum = _mm_exact_lhs(tril_ref[...], ld)
        yield
        kk = kk * lax.rsqrt(jnp.maximum(kk_ss, 1e-24))
        bv = kk * a
        e_cum = jnp.exp(cum)
        e_neg = jnp.exp(-cum)
        chunk_start = lax.broadcasted_iota(jnp.int32, cum.shape, 0) % CHUNK == 0
        e_prev = jnp.where(chunk_start, 1.0, pltpu.roll(e_cum, 1, axis=0))
        pro.update(at=-kk * e_prev, rt=r * e_cum, bt=bv * e_neg, kt=k2 * e_neg, bv=bv, k2=k2, v=v,
                   e_cum=e_cum, e_neg=e_neg, gate=gate, bonus=bonus * v)

    def algebra(pro, res):
        chains = [(slice(ch * CHUNK, (ch + 1) * CHUNK), slice(j * PAIR, (j + 1) * PAIR), (ch + 1) * CHUNK - 1)
                  for ch in range(sub // CHUNK) for j in range(dr // PAIR)]
        atp = [pro["at"][rs, sl] for rs, sl, _ in chains]
        rtp = [pro["rt"][rs, sl] for rs, sl, _ in chains]
        vp = [pro["v"][rs, sl] for rs, sl, _ in chains]
        g0, g1 = [], []
        for n, (rs, sl, _) in enumerate(chains):
            btp, ktp = pro["bt"][rs, sl], pro["kt"][rs, sl]
            lhs = jnp.concatenate([atp[n], rtp[n]], axis=0)
            g0.append(_mm(jnp.where(m0w, lhs, zero2), jnp.concatenate([btp, ktp], axis=0), NT))
            g1.append(_mm(jnp.where(m0w, zero2, lhs), jnp.concatenate([ktp, btp], axis=0), NT))
        yield
        qp = [jnp.where(strict, jnp.where(m0, a[:CHUNK], b[:CHUNK]), zero) for a, b in zip(g0, g1)]
        akmk = [jnp.concatenate([jnp.where(strict, jnp.where(m0, b[:CHUNK], a[:CHUNK]), zero),
                                 jnp.where(incl, jnp.where(m0, b[CHUNK:], a[CHUNK:]), zero)], axis=0)
                for a, b in zip(g0, g1)]
        mb = [jnp.where(incl, jnp.where(m0, a[CHUNK:], b[CHUNK:]), zero) for a, b in zip(g0, g1)]
        kv = [_mm(l, bd(t, True)) for l, t in zip(akmk, vp)]
        yield
        tm = [eye2 + t for t in qp]
        bq = [bd(t) for t in qp]
        for _ in range(5):
            qp = [_mm(t, b) for t, b in zip(qp, bq)]
            yield
            bq = [bd(t) for t in qp]
            tm = [t + _mm(t, b) for t, b in zip(tm, bq)]
            yield
        au = [_mm(t, jnp.concatenate([bd(a), bd(k[:CHUNK])], axis=1)) for t, a, k in zip(tm, atp, kv)]
        yield
        ry = [_mm(t, jnp.concatenate([bd(a[:, :PAIR]), bd(a[:, PAIR:])], axis=1)) for t, a in zip(mb, au)]
        ph = []
        for n, (rs, sl, last) in enumerate(chains):
            e_end = pro["e_cum"][last:last + 1, sl] * pro["e_neg"][rs, sl]
            rhs = jnp.concatenate([au[n], jnp.concatenate([zero, vp[n]], axis=1)], axis=0)
            ph.append(_mm(jnp.concatenate([pro["bv"][rs, sl] * e_end, pro["k2"][rs, sl] * e_end], axis=0), rhs, TN))
        yield
        res["rp"] = [(rtp[n] + ry[n][:, :PAIR]).astype(BF16) for n in range(len(chains))]
        res["y0"] = [kv[n][CHUNK:] + ry[n][:, PAIR:] for n in range(len(chains))]
        res["pc"] = [(jnp.where(m0, ph[n][:CHUNK, :PAIR], ph[n][CHUNK:, :PAIR])
                      + jnp.where(diag, pro["e_cum"][last:last + 1, sl], 0.0)).astype(BF16)
                     for n, (rs, sl, last) in enumerate(chains)]
        res["hinc"] = [jnp.where(m0, ph[n][:CHUNK, PAIR:], ph[n][CHUNK:, PAIR:]) for n in range(len(chains))]

    npair = dr // PAIR

    def scan(i, pro, res, hs):
        ys = []
        for ch in range(sub // CHUNK):
            for j in range(npair):
                n = ch * npair + j
                out = _mm(jnp.concatenate([res["rp"][n], res["pc"][n]], axis=0), _bd2(hs[j], m0))
                ys.append(res["y0"][n] + out[:CHUNK])
                hs[j] = res["hinc"][n] + out[CHUNK:]
            yield
        y = jnp.concatenate([jnp.concatenate(ys[ch * npair:(ch + 1) * npair], axis=1)
                             for ch in range(sub // CHUNK)], axis=0)
        mean = _segsum(y, ones) * (1.0 / HEAD)
        d = y - mean
        var = _segsum(d * d, ones) * (1.0 / HEAD)
        yn = d * lax.rsqrt(var + GN_EPS) * gnw_ref[...] + gnb_ref[...]
        y_ref[i * sub:(i + 1) * sub, :] = ((yn + pro["bonus"]) * pro["gate"]).astype(BF16)

    nsub = ps_ref.shape[0] // sub
    pros = [dict() for _ in range(nsub)]
    hs = [h_scr[:, j * PAIR:(j + 1) * PAIR] for j in range(npair)]
    for _ in prologue(0, pros[0]):
        pass
    behind = iter(())
    for i in range(nsub):
        ahead = prologue(i + 1, pros[i + 1]) if i + 1 < nsub else iter(())
        res = {}
        for stage, _ in enumerate(algebra(pros[i], res)):
            next(ahead if stage % 2 == 0 else behind, None)
        for _ in ahead:
            pass
        for _ in behind:
            pass
        behind = scan(i, pros[i], res, hs)
    for _ in behind:
        pass
    for j in range(npair):
        h_scr[:, j * PAIR:(j + 1) * PAIR] = hs[j]
    hout_ref[...] = h_scr[...]


def _wkv(ps, shift_prev, h0, lp, ones_pair, batch, seq):
    n, ns = ps.shape
    dr = lp["w0"].shape[-1]
    nd = lp["w_decay"].shape[0]
    na = lp["w_aaa"].shape[0]
    rows = min(WKV_ROWS, seq)
    sub = min(WKV_SUB, rows)
    nc = seq // rows
    blk = rows // 8
    tril = jnp.kron(jnp.eye(sub // CHUNK, dtype=F32), jnp.tril(jnp.ones((CHUNK, CHUNK), F32))).astype(BF16)
    row = lambda w: pl.BlockSpec((rows, w), lambda b, c: (b * nc + c, 0))
    st = pl.BlockSpec((None, HEAD, dr), lambda b, c: (b, 0, 0))
    cs = lambda a: _const_spec(a.shape, 2)
    consts = [lp["mu"], lp["w0"], lp["a0"], lp["k_k"], lp["k_a"], lp["r_k"],
              lp["w_decay"], lp["w_aaa"], lp["w_gate"], ones_pair, tril]
    return pl.pallas_call(
        functools.partial(_wkv_body, dr=dr, nd=nd, na=na, sub=sub),
        grid=(batch, nc),
        in_specs=[row(ns),
                  pl.BlockSpec((8, ns), lambda b, c: (jnp.maximum((b * nc + c) * blk - 1, 0), 0)),
                  pl.BlockSpec((None, 1, ns), lambda b, c: (b, 0, 0))] + [cs(a) for a in consts]
                 + [st, cs(lp["gn_w"]), cs(lp["gn_b"])],
        out_specs=[row(dr), st],
        out_shape=[jax.ShapeDtypeStruct((n, dr), BF16), jax.ShapeDtypeStruct((batch, HEAD, dr), F32)],
        scratch_shapes=[pltpu.VMEM((HEAD, dr), F32)],
        compiler_params=_params("arbitrary", "arbitrary"),
        name="wkv",
    )(ps, ps, shift_prev, *consts, h0, lp["gn_w"], lp["gn_b"])


def _flash_init(m_scr, acc_scr):
    m_scr[...] = jnp.full(m_scr.shape, NEG_BIG, F32)
    acc_scr[...] = jnp.zeros(acc_scr.shape, F32)


def _lane_rep(x, width):
    if width % PAIR == 0:
        return x if width == PAIR else jnp.concatenate([x] * (width // PAIR), axis=1)
    return x[:, :width]


def _flash_update(q, k_of, v_of, tk, m_scr, acc_scr, mask=None, rows=slice(None)):
    tq = q.shape[0]
    first = lax.broadcasted_iota(jnp.int32, (tq, PAIR), 1) < HEAD
    zq = jnp.zeros((tq, PAIR), BF16)
    ones = jnp.ones((tk, PAIR), BF16)
    n_maps = 2 * (q.shape[1] // PAIR)

    def scores(i):
        qh = q[:, (i // 2) * PAIR:(i // 2 + 1) * PAIR]
        qm = jnp.where(first, qh, zq) if i % 2 == 0 else jnp.where(first, zq, qh)
        return _dg(qm, k_of(i // 2), NT)

    s_next = scores(0)
    for i in range(n_maps):
        s = s_next
        if i + 1 < n_maps:
            s_next = scores(i + 1)
        if mask is not None:
            s = jnp.where(mask, s, NEG_BIG)
        v_aug = jnp.concatenate([v_of(i // 2), ones], axis=1)
        m_prev = m_scr[i, rows, :]
        m_new = jnp.maximum(m_prev, jnp.max(s, axis=1, keepdims=True))
        alpha = jnp.exp2(m_prev - m_new)
        p = jnp.exp2((s - _lane_rep(m_new, tk)).astype(BF16))
        acc_scr[i, rows, :] = _lane_rep(alpha, 2 * PAIR) * acc_scr[i, rows, :] + _dg(p, v_aug, NN)
        m_scr[i, rows, :] = m_new


def _flash_finish(lam, sub_ref, scale, y_ref, acc_scr):
    for h in range(y_ref.shape[1] // PAIR):
        a1, a2 = acc_scr[2 * h], acc_scr[2 * h + 1]
        o = a1[:, :PAIR] / a1[:, PAIR:] - lam * (a2[:, :PAIR] / a2[:, PAIR:])
        ms = jnp.mean(o * o, axis=-1, keepdims=True)
        y = o * lax.rsqrt(ms + NORM_EPS) * sub_ref[...] * scale
        y_ref[:, h * PAIR:(h + 1) * PAIR] = y.astype(BF16)


def _attn_prompt_body(qt_ref, kt_ref, lam_ref, q_ref, k_ref, v_ref, sub_ref, y_ref, m_scr, acc_scr, *, tq, scale):
    s = pl.program_id(1)
    qi = qt_ref[s]
    kj = kt_ref[s]
    head = lambda h: slice(h * PAIR, (h + 1) * PAIR)

    @pl.when(kj == 0)
    def _():
        _flash_init(m_scr, acc_scr)

    nkt = k_ref.shape[0] // tq
    last_tile = kj * nkt + nkt - 1

    @pl.when(last_tile < qi)
    def _():
        _flash_update(q_ref[...], lambda h: k_ref[:, head(h)], lambda h: v_ref[:, head(h)], nkt * tq,
                      m_scr, acc_scr)

    for j in range(nkt):
        rows = slice(j * tq, (j + 1) * tq)
        k_of = lambda h, rows=rows: k_ref[rows, head(h)]
        v_of = lambda h, rows=rows: v_ref[rows, head(h)]
        tile = kj * nkt + j

        if j < nkt - 1:
            @pl.when((last_tile >= qi) & (tile < qi))
            def _(k_of=k_of, v_of=v_of):
                _flash_update(q_ref[...], k_of, v_of, tq, m_scr, acc_scr)

        @pl.when(tile == qi)
        def _(j=j):
            ng = 2 if tq % (2 * CHUNK) == 0 else 1
            grp = tq // ng
            for g in range(ng):
                nkeys = (g + 1) * grp
                keys = slice(j * tq, j * tq + nkeys)
                qrow = g * grp + lax.broadcasted_iota(jnp.int32, (grp, nkeys), 0)
                kcol = lax.broadcasted_iota(jnp.int32, (grp, nkeys), 1)
                mask = (kcol // CHUNK) <= (qrow // CHUNK)
                qrows = slice(g * grp, (g + 1) * grp)
                _flash_update(q_ref[qrows, :], lambda h, keys=keys: k_ref[keys, head(h)],
                              lambda h, keys=keys: v_ref[keys, head(h)], nkeys, m_scr, acc_scr, mask, qrows)
            _flash_finish(lam_ref[0], sub_ref, scale, y_ref, acc_scr)


def _attn_prompt(lam, q, kb, vb, subln, scale, batch, seq):
    n, dd = q.shape
    tq = min(ATTN_TILE, seq)
    nq = seq // tq
    nh2 = 2 * dd // PAIR
    nkt = 2 if nq % 2 == 0 else 1
    pairs = [(i, j) for i in range(nq) for j in range(i // nkt + 1)]
    qt = jnp.array([p[0] for p in pairs], jnp.int32)
    kt = jnp.array([p[1] for p in pairs], jnp.int32)
    kv_spec = pl.BlockSpec((tq * nkt, dd), lambda b, s, qt, kt: (b * (nq // nkt) + kt[s], 0))
    q_spec = pl.BlockSpec((tq, dd), lambda b, s, qt, kt: (b * nq + qt[s], 0))
    return pl.pallas_call(
        functools.partial(_attn_prompt_body, tq=tq, scale=scale),
        grid_spec=pltpu.PrefetchScalarGridSpec(
            num_scalar_prefetch=2,
            grid=(batch, len(pairs)),
            in_specs=[pl.BlockSpec(memory_space=pltpu.SMEM), q_spec, kv_spec, kv_spec,
                      pl.BlockSpec((1, PAIR), lambda b, s, qt, kt: (0, 0))],
            out_specs=q_spec,
            scratch_shapes=[pltpu.VMEM((nh2, tq, PAIR), F32), pltpu.VMEM((nh2, tq, 2 * PAIR), F32)]),
        out_shape=jax.ShapeDtypeStruct((n, dd), BF16),
        compiler_params=_params("arbitrary", "arbitrary"),
        name="attn_prompt",
    )(qt, kt, lam, q, kb, vb, subln)


def _attn_sample_body(lam_ref, q_ref, ck_ref, cv_ref, k_ref, v_ref, sub_ref, y_ref, m_scr, acc_scr, *, scale):
    kj = pl.program_id(1)
    last = pl.num_programs(1) - 1
    head = lambda h: slice(h * PAIR, (h + 1) * PAIR)
    nh = q_ref.shape[1] // PAIR
    tk = ck_ref.shape[0] // nh

    @pl.when(kj == 0)
    def _():
        _flash_init(m_scr, acc_scr)

    @pl.when(kj < last)
    def _():
        _flash_update(q_ref[...], lambda h: ck_ref[pl.ds(h, tk, stride=nh), :].astype(BF16),
                      lambda h: cv_ref[pl.ds(h, tk, stride=nh), :].astype(BF16), tk, m_scr, acc_scr)

    @pl.when(kj == last)
    def _():
        _flash_update(q_ref[...], lambda h: k_ref[:, head(h)], lambda h: v_ref[:, head(h)],
                      k_ref.shape[0], m_scr, acc_scr)
        _flash_finish(lam_ref[0], sub_ref, scale, y_ref, acc_scr)


def _attn_sample(lam, q, cache_k, cache_v, layer, kb, vb, subln, scale, batch, seq):
    n, dd = q.shape
    past = cache_k.shape[2]
    tk = min(CACHE_TILE, past)
    nk = past // tk
    nh2 = 2 * dd // PAIR
    row = pl.BlockSpec((seq, dd), lambda b, j: (b, 0))
    nh = dd // PAIR
    cache_k = cache_k.reshape(cache_k.shape[:2] + (past * nh, PAIR))
    cache_v = cache_v.reshape(cache_v.shape[:2] + (past * nh, PAIR))
    cache = pl.BlockSpec((None, None, tk * nh, PAIR), lambda b, j: (layer, b, jnp.minimum(j, nk - 1), 0))
    return pl.pallas_call(
        functools.partial(_attn_sample_body, scale=scale),
        grid=(batch, nk + 1),
        in_specs=[pl.BlockSpec(memory_space=pltpu.SMEM), row, cache, cache, row, row, _const_spec((1, PAIR), 2)],
        out_specs=row,
        out_shape=jax.ShapeDtypeStruct((n, dd), BF16),
        scratch_shapes=[pltpu.VMEM((nh2, seq, PAIR), F32), pltpu.VMEM((nh2, seq, 2 * PAIR), F32)],
        compiler_params=_params("arbitrary", "arbitrary"),
        name="attn_sample",
    )(lam, q, cache_k, cache_v, kb, vb, subln)


def _ffn_body(h_ref, yr_ref, ya_ref, wo_ref, gf_ref, wup_ref, cw_ref, cb_ref, wdn_ref, cprev_ref,
              hout_ref, cnew_ref, carry_scr, *, dff, fc):
    t = pl.program_id(1)

    @pl.when(t == 0)
    def _():
        carry_scr[...] = cprev_ref[...]

    y = jnp.concatenate([yr_ref[...], ya_ref[...]], axis=1)
    h1 = h_ref[...] + _dg(y, wo_ref[...], NN)
    ms = jnp.mean(h1 * h1, axis=-1, keepdims=True)
    xn = (h1 * lax.rsqrt(ms + NORM_EPS) * gf_ref[...]).astype(BF16)
    tm = h1.shape[0]
    rowi = lax.broadcasted_iota(jnp.int32, (tm, fc), 0)
    acc = jnp.zeros(h1.shape, F32)

    def up(c):
        return (_dg(xn, wup_ref[:, c * fc:(c + 1) * fc], NN),
                _dg(xn, wup_ref[:, dff + c * fc:dff + (c + 1) * fc], NN))

    nxt = up(0)
    hs = []
    for c in range(dff // fc):
        sl = slice(c * fc, (c + 1) * fc)
        gt, u = nxt
        if c + 1 < dff // fc:
            nxt = up(c + 1)
        p1 = carry_scr[7:8, sl]
        p2 = carry_scr[6:7, sl]
        g1 = jnp.where(rowi == 0, p1, pltpu.roll(gt, 1, axis=0))
        g2 = jnp.where(rowi == 0, p2, jnp.where(rowi == 1, p1, pltpu.roll(gt, 2, axis=0)))
        gc = cb_ref[:, sl] + g2 * cw_ref[0:1, sl] + g1 * cw_ref[1:2, sl] + gt * cw_ref[2:3, sl]
        carry_scr[:, sl] = gt[tm - 8:, :]
        hs.append((gc * jax.nn.sigmoid(gc) * u).astype(BF16))
        if len(hs) == DOWN_GROUP or c + 1 == dff // fc:
            lo = (c + 1 - len(hs)) * fc
            acc = acc + _dg(jnp.concatenate(hs, axis=1), wdn_ref[lo:(c + 1) * fc, :], NN)
            hs = []
    hout_ref[...] = h1 + acc
    cnew_ref[...] = carry_scr[...]


def _ffn(h, yr, ya, wo, g_ffn, wup, cw, cb, wdn, conv_prev8, batch, seq):
    n, d = h.shape
    dff = cb.shape[-1]
    fc = FFN_COLS
    tm = min(ROW_TILE, seq)
    nt = seq // tm
    row = lambda w: pl.BlockSpec((tm, w), lambda b, t: (b * nt + t, 0))
    st = pl.BlockSpec((None, 8, dff), lambda b, t: (b, 0, 0))
    cs = lambda a: _const_spec(a.shape, 2)
    return pl.pallas_call(
        functools.partial(_ffn_body, dff=dff, fc=fc),
        grid=(batch, nt),
        in_specs=[row(d), row(yr.shape[1]), row(ya.shape[1]), cs(wo), cs(g_ffn), cs(wup), cs(cw), cs(cb), cs(wdn), st],
        out_specs=[row(d), st],
        out_shape=[jax.ShapeDtypeStruct((n, d), F32), jax.ShapeDtypeStruct((batch, 8, dff), F32)],
        scratch_shapes=[pltpu.VMEM((8, dff), F32)],
        compiler_params=_params("arbitrary", "arbitrary"),
        name="ffn",
    )(h, yr, ya, wo, g_ffn, wup, cw, cb, wdn, conv_prev8)


def _lambda_init(layer):
    return 0.8 - 0.6 * math.exp(-0.3 * layer)


def _run_group(x, depth, layers, ones_pair, shift_prev, wkv_prev, conv_prev, attend):
    batch, seq, d = x.shape
    h = x.reshape(batch * seq, d)
    ws, ss, cs = [], [], []
    nh = layers[0]["q_gain"].shape[-1] // PAIR
    leaves = None
    for l in range(depth):
        lp = layers[l]
        ns = lp["mu"].shape[-1]
        dd = lp["q_gain"].shape[-1]
        dr = lp["w0"].shape[-1]
        dff = lp["conv_b"].shape[-1]
        ps, q, k_all, v_all, kb, vb = _inproj(h, lp["g_mix"], lp["w_in"], lp["q_gain"], lp["k_gain"],
                                              leaves, l, depth, ns, dd)
        leaves = (k_all, v_all)

        sp = jnp.zeros((batch, 1, ns), F32) if shift_prev is None else shift_prev[l][:, None, :]
        if wkv_prev is None:
            h0 = jnp.zeros((batch, HEAD, dr), F32)
        else:
            h0 = jnp.transpose(wkv_prev[l], (0, 3, 1, 2)).reshape(batch, HEAD, dr)
        yr, hout = _wkv(ps, sp, h0, lp, ones_pair, batch, seq)

        lam_init = _lambda_init(l)
        ya = attend(l, lp["lam"], q, kb, vb, lp["subln"], 1.0 - lam_init, batch, seq)

        cp = jnp.zeros((batch, 8, dff), F32) if conv_prev is None else jnp.pad(conv_prev[l], ((0, 0), (6, 0), (0, 0)))
        h, cnew = _ffn(h, yr, ya, lp["w_out"], lp["g_ffn"], lp["w_up"], lp["conv_w"], lp["conv_b"], lp["w_down"],
                       cp, batch, seq)

        ws.append(jnp.transpose(hout.reshape(batch, HEAD, dr // HEAD, HEAD), (0, 2, 3, 1)))
        ss.append(ps.reshape(batch, seq, ns)[:, -1])
        cs.append(cnew[:, 6:8])
    leaf = lambda t: t.reshape(depth, batch, seq, nh, PAIR)
    return (h.reshape(batch, seq, d), leaf(k_all), leaf(v_all), jnp.stack(ws), jnp.stack(ss), jnp.stack(cs))


def kernel(x_prompt, x_sample, cache_k, cache_v, state_wkv, state_shift, state_conv, g_mix, w_in, mu_shift, w0, w_decay, a0, w_aaa, w_gate, k_k, k_a, r_k, gn_w, gn_b, q_gain, k_gain, lambdas, subln_gain, w_out, g_ffn, w_ffn_in, conv_w, conv_b, w_ffn_out):
    depth = w_in.shape[0]
    dr = w0.shape[-1]
    ns = mu_shift.shape[-1]
    dd = (w_in.shape[-1] - ns) // 3
    assert dr % PAIR == 0 and dd % PAIR == 0 and q_gain.shape[-1] == HEAD and subln_gain.shape[-1] == PAIR
    assert x_prompt.shape[1] % CHUNK == 0 and x_sample.shape[1] == CHUNK

    row = lambda a: a.reshape(1, -1).astype(F32)
    layers = []
    for l in range(depth):
        lv = lambdas[l].astype(F32)
        lam = jnp.exp(jnp.sum(lv[0] * lv[1])) - jnp.exp(jnp.sum(lv[2] * lv[3])) + _lambda_init(l)
        layers.append(dict(
            g_mix=row(g_mix[l]), w_in=w_in[l].astype(BF16), mu=row(mu_shift[l]), w0=row(w0[l]), a0=row(a0[l]),
            w_decay=w_decay[l], w_aaa=w_aaa[l], w_gate=w_gate[l], k_k=row(k_k[l]), k_a=row(k_a[l]), r_k=row(r_k[l]),
            gn_w=row(gn_w[l]), gn_b=row(gn_b[l]),
            q_gain=row(jnp.tile(q_gain[l], dd // HEAD)), k_gain=row(jnp.tile(k_gain[l], dd // HEAD)),
            lam=lam.reshape(1), subln=row(subln_gain[l]), w_out=w_out[l].astype(BF16), g_ffn=row(g_ffn[l]),
            w_up=w_ffn_in[l].astype(BF16), conv_w=conv_w[l], conv_b=row(conv_b[l]), w_down=w_ffn_out[l].astype(BF16)))

    ones_pair = jnp.kron(jnp.eye(2, dtype=F32), jnp.ones((HEAD, HEAD), F32)).astype(BF16)

    def attend_prompt(l, lam, q, kb, vb, subln, scale, batch, seq):
        return _attn_prompt(lam, q, kb, vb, subln, scale, batch, seq)

    def attend_sample(l, lam, q, kb, vb, subln, scale, batch, seq):
        return _attn_sample(lam, q, cache_k, cache_v, l, kb, vb, subln, scale, batch, seq)

    yp, pk, pv, pw, ps_, pc = _run_group(x_prompt, depth, layers, ones_pair, None, None, None, attend_prompt)
    ys, sk, sv, sw, ss, sc = _run_group(x_sample, depth, layers, ones_pair, state_shift, state_wkv, state_conv,
                                        attend_sample)
    return (yp, ys, pk, pv, pw, ps_, pc, sk, sv, sw, ss, sc)
```
